```python
import math
import jax, jax.numpy as jnp
from jax import lax
import numpy as np

D_MODEL = 1024
BATCH = 4
SEQ = 8192
DEPTH = 2

HEAD_DIM = 64
N_HEADS_A = 4
DIFF_DIM = HEAD_DIM // 2
N_HEADS_B = 4
N_HEADS_C = 4
N_HEADS_D = 4
BRANCH_WIDTH = 4 * HEAD_DIM
N_BRANCHES = 4
ROPE_THETA = 500000.0
ROPE_FRACTION = 4
Q_BLOCK = 128
DILATED_PATTERNS = ((128, 1), (512, 4), (2048, 16))
D_FF = 2048
CONV_WIDTH = 3
LN_EPS = 1e-5
DEEPNORM_ALPHA = (2.0 * DEPTH) ** 0.25
DEEPNORM_BETA = (8.0 * DEPTH) ** -0.25

kernel_name = 'hybrid_gated_parallel_mixers_deepnorm'


def _layer_norm(x, g, b):
    xf = x.astype(jnp.float32)
    mu = jnp.mean(xf, axis=-1, keepdims=True)
    var = jnp.mean(jnp.square(xf - mu), axis=-1, keepdims=True)
    return ((xf - mu) * lax.rsqrt(var + LN_EPS)).astype(x.dtype) * g + b


def _rms(x):
    xf = x.astype(jnp.float32)
    return (xf * lax.rsqrt(jnp.mean(jnp.square(xf), axis=-1, keepdims=True) + LN_EPS)).astype(x.dtype)


def _heads(t, n):
    b, s, w = t.shape
    return t.reshape(b, s, n, w // n).transpose(0, 2, 1, 3)


def _merge_heads(t):
    b, n, s, d = t.shape
    return t.transpose(0, 2, 1, 3).reshape(b, s, n * d)


def _blocks(t):
    b, h, s, d = t.shape
    return jnp.moveaxis(t.reshape(b, h, s // Q_BLOCK, Q_BLOCK, d), 2, 0)


def _unblocks(t):
    nb, b, h, q, d = t.shape
    return jnp.moveaxis(t, 0, 2).reshape(b, h, nb * q, d)


def _partial_rotary(x, positions):
    dh = x.shape[-1]
    r = dh // ROPE_FRACTION
    half = r // 2
    inv_freq = jnp.power(jnp.float32(ROPE_THETA), -2.0 * jnp.arange(half, dtype=jnp.float32) / r)
    ang = positions.astype(jnp.float32)[:, None, :, None] * inv_freq
    cos, sin = jnp.cos(ang), jnp.sin(ang)
    xr = x[..., :r].astype(jnp.float32)
    x1, x2 = xr[..., :half], xr[..., half:]
    rot = jnp.concatenate([x1 * cos - x2 * sin, x2 * cos + x1 * sin], axis=-1).astype(x.dtype)
    return jnp.concatenate([rot, x[..., r:]], axis=-1)


def _diff_attention(q, k, v, lam):
    b, h2, s, dd = q.shape
    h = h2 // 2
    scale = dd ** -0.5
    kpos = jnp.arange(s)

    def block(args):
        qb, n = args
        sc = jnp.einsum('bhqd,bhkd->bhqk', qb, k).astype(jnp.float32) * scale
        qpos = n * Q_BLOCK + jnp.arange(Q_BLOCK)
        sc = jnp.where(kpos[None, :] <= qpos[:, None], sc, -jnp.inf)
        p = jax.nn.softmax(sc, axis=-1).reshape(b, h, 2, Q_BLOCK, s)
        w = p[:, :, 0] - lam * p[:, :, 1]
        return jnp.einsum('bhqk,bhkd->bhqd', w.astype(v.dtype), v)

    return _unblocks(lax.map(block, (_blocks(q), jnp.arange(s // Q_BLOCK))))


def _stick_breaking_attention(q, k, v):
    b, h, s, d = q.shape
    scale = d ** -0.5
    kpos = jnp.arange(s)

    def block(args):
        qb, n = args
        z = jnp.einsum('bhqd,bhkd->bhqk', qb, k).astype(jnp.float32) * scale
        qpos = n * Q_BLOCK + jnp.arange(Q_BLOCK)
        strict = kpos[None, :] < qpos[:, None]
        log_keep = jnp.where(strict, jax.nn.log_sigmoid(-z), 0.0)
        after = lax.cumsum(log_keep, axis=log_keep.ndim - 1, reverse=True) - log_keep
        a = jnp.where(strict, jnp.exp(jax.nn.log_sigmoid(z) + after), 0.0)
        return jnp.einsum('bhqk,bhkd->bhqd', a.astype(v.dtype), v)

    return _unblocks(lax.map(block, (_blocks(q), jnp.arange(s // Q_BLOCK))))


def _banded_attention(q, k, v, steps):
    *lead, L, d = q.shape
    nb = L // Q_BLOCK
    qb = q.reshape(*lead, nb, Q_BLOCK, d)
    kb = k.reshape(*lead, nb, Q_BLOCK, d)
    vb = v.reshape(*lead, nb, Q_BLOCK, d)
    k_prev = jnp.concatenate([jnp.zeros_like(kb[..., :1, :, :]), kb[..., :-1, :, :]], axis=-3)
    v_prev = jnp.concatenate([jnp.zeros_like(vb[..., :1, :, :]), vb[..., :-1, :, :]], axis=-3)
    kw = jnp.concatenate([k_prev, kb], axis=-2)
    vw = jnp.concatenate([v_prev, vb], axis=-2)
    sc = jnp.einsum('...nqd,...nkd->...nqk', qb, kw).astype(jnp.float32) * (d ** -0.5)
    i = jnp.arange(Q_BLOCK)[:, None]
    j = jnp.arange(2 * Q_BLOCK)[None, :]
    dist = Q_BLOCK + i - j
    blk = jnp.arange(nb)[:, None, None]
    valid = (dist >= 0) & (dist <= steps) & ((blk > 0) | (j >= Q_BLOCK))
    sc = jnp.where(valid, sc, -jnp.inf)
    m = jnp.max(sc, axis=-1, keepdims=True)
    p = jnp.exp(sc - m)
    den = jnp.sum(p, axis=-1, keepdims=True)
    out = jnp.einsum('...nqk,...nkd->...nqd', (p / den).astype(v.dtype), vw)
    lse = (m + jnp.log(den))[..., 0]
    return out.reshape(*lead, L, d), lse.reshape(*lead, L)


def _dilated_attention(q, k, v):
    b, h, s, d = q.shape
    outs, lses = [], []
    for window, dil in DILATED_PATTERNS:
        steps = window // dil
        L = s // dil
        Lp = -(-L // Q_BLOCK) * Q_BLOCK

        def strided(t):
            t = t.reshape(b, h, L, dil, d).transpose(0, 1, 3, 2, 4)
            return jnp.pad(t, ((0, 0), (0, 0), (0, 0), (0, Lp - L), (0, 0)))

        o, lse = _banded_attention(strided(q), strided(k), strided(v), steps)
        outs.append(o[..., :L, :].transpose(0, 1, 3, 2, 4).reshape(b, h, s, d))
        lses.append(lse[..., :L].transpose(0, 1, 3, 2).reshape(b, h, s))
    w = jax.nn.softmax(jnp.stack(lses, axis=0), axis=0)
    return jnp.sum(w[..., None].astype(v.dtype) * jnp.stack(outs, axis=0), axis=0)


def _forgetting_attention(q, k, v, log_f):
    b, h, s, d = q.shape
    scale = d ** -0.5
    F = lax.cumsum(log_f, axis=log_f.ndim - 1)
    kpos = jnp.arange(s)

    def block(args):
        qb, Fq, n = args
        sc = jnp.einsum('bhqd,bhkd->bhqk', qb, k).astype(jnp.float32) * scale
        sc = sc + Fq[..., 0][..., None] - F[:, :, None, :]
        qpos = n * Q_BLOCK + jnp.arange(Q_BLOCK)
        sc = jnp.where(kpos[None, :] <= qpos[:, None], sc, -jnp.inf)
        p = jax.nn.softmax(sc, axis=-1)
        return jnp.einsum('bhqk,bhkd->bhqd', p.astype(v.dtype), v)

    return _unblocks(lax.map(block, (_blocks(q), _blocks(F[..., None]), jnp.arange(s // Q_BLOCK))))


def _mixer_sublayer(h, positions, lambda_init, w_in, lam_q1, lam_k1, lam_q2, lam_k2,
                    subln_g, forget_b, w_branch, w_o):
    b, s, _ = h.shape
    proj = jnp.einsum('bsd,de->bse', h, w_in)
    widths = [BRANCH_WIDTH] * 12 + [N_HEADS_D, N_BRANCHES * D_MODEL]
    cuts = [int(t) for t in np.cumsum(widths)[:-1]]
    (qa, ka, va, qb, kb, vb, qc, kc, vc, qd, kd, vd, f_logit, gate_logit) = jnp.split(proj, cuts, axis=-1)

    lam = jnp.exp(jnp.sum(lam_q1 * lam_k1)) - jnp.exp(jnp.sum(lam_q2 * lam_k2)) + lambda_init
    oa = _diff_attention(_partial_rotary(_heads(qa, 2 * N_HEADS_A), positions),
                         _partial_rotary(_heads(ka, 2 * N_HEADS_A), positions),
                         _heads(va, N_HEADS_A), lam)
    oa = _rms(oa) * subln_g * (1.0 - lambda_init)

    ob = _stick_breaking_attention(_heads(qb, N_HEADS_B), _heads(kb, N_HEADS_B), _heads(vb, N_HEADS_B))

    oc = _dilated_attention(_partial_rotary(_heads(qc, N_HEADS_C), positions),
                            _partial_rotary(_heads(kc, N_HEADS_C), positions),
                            _heads(vc, N_HEADS_C))

    log_f = jax.nn.log_sigmoid((f_logit + forget_b).astype(jnp.float32)).transpose(0, 2, 1)
    od = _forgetting_attention(_heads(qd, N_HEADS_D), _heads(kd, N_HEADS_D), _heads(vd, N_HEADS_D), log_f)

    branches = jnp.stack([_merge_heads(o) for o in (oa, ob, oc, od)], axis=2)
    y = jnp.einsum('bsnc,ncd->bsnd', branches, w_branch)
    gates = jax.nn.sigmoid(gate_logit).reshape(b, s, N_BRANCHES, D_MODEL)
    merged = jnp.sum(gates * y, axis=2)
    return jnp.einsum('bsd,de->bse', merged, w_o)


def _conv_ffn(h, w_up, conv_w, conv_b, w_down):
    u = jnp.einsum('bsd,df->bsf', h, w_up)
    u = lax.conv_general_dilated(u, conv_w[:, None, :], window_strides=(1,),
                                 padding=((CONV_WIDTH - 1, 0),),
                                 dimension_numbers=('NWC', 'WIO', 'NWC'),
                                 feature_group_count=2 * D_FF) + conv_b
    a, g = jnp.split(u, 2, axis=-1)
    return jnp.einsum('bsf,fd->bsd', jax.nn.silu(a) * g, w_down)


def setup_inputs(seed: int = 0) -> dict:
    key = jax.random.key(seed)
    ks = jax.random.split(key, 24)
    f32 = jnp.float32
    n = lambda k, shape: jax.random.normal(k, shape, dtype=f32)
    in_cols = 12 * BRANCH_WIDTH + N_HEADS_D + N_BRANCHES * D_MODEL
    x = n(ks[0], (BATCH, SEQ, D_MODEL))
    c = n(ks[1], (BATCH, D_MODEL))
    positions = (jnp.arange(SEQ, dtype=jnp.int32)[None, :]
                 + jax.random.randint(ks[2], (BATCH, 1), 0, 1024, dtype=jnp.int32))
    return {
        'x': x,
        'c': c,
        'positions': positions,
        'w_ada': n(ks[3], (DEPTH, D_MODEL, 6 * D_MODEL)) * D_MODEL ** -0.5,
        'b_ada': 0.02 * n(ks[4], (DEPTH, 6 * D_MODEL)),
        'w_in': n(ks[5], (DEPTH, D_MODEL, in_cols)) * D_MODEL ** -0.5,
        'lam_q1': 0.1 * n(ks[6], (DEPTH, DIFF_DIM)),
        'lam_k1': 0.1 * n(ks[7], (DEPTH, DIFF_DIM)),
        'lam_q2': 0.1 * n(ks[8], (DEPTH, DIFF_DIM)),
        'lam_k2': 0.1 * n(ks[9], (DEPTH, DIFF_DIM)),
        'subln_g': 1.0 + 0.02 * n(ks[10], (DEPTH, HEAD_DIM)),
        'forget_b': 4.0 + 0.5 * n(ks[11], (DEPTH, N_HEADS_D)),
        'w_branch': n(ks[12], (DEPTH, N_BRANCHES, BRANCH_WIDTH, D_MODEL)) * BRANCH_WIDTH ** -0.5,
        'w_o': n(ks[13], (DEPTH, D_MODEL, D_MODEL)) * D_MODEL ** -0.5 * DEEPNORM_BETA,
        'ln1_g': 1.0 + 0.02 * n(ks[14], (DEPTH, D_MODEL)),
        'ln1_b': 0.02 * n(ks[15], (DEPTH, D_MODEL)),
        'w_up': n(ks[16], (DEPTH, D_MODEL, 2 * D_FF)) * D_MODEL ** -0.5,
        'conv_w': n(ks[17], (DEPTH, CONV_WIDTH, 2 * D_FF)) * CONV_WIDTH ** -0.5,
        'conv_b': 0.02 * n(ks[18], (DEPTH, 2 * D_FF)),
        'w_down': n(ks[19], (DEPTH, D_FF, D_MODEL)) * D_FF ** -0.5 * DEEPNORM_BETA,
        'ln2_g': 1.0 + 0.02 * n(ks[20], (DEPTH, D_MODEL)),
        'ln2_b': 0.02 * n(ks[21], (DEPTH, D_MODEL)),
    }


def reference(x, c, positions, w_ada, b_ada, w_in, lam_q1, lam_k1, lam_q2, lam_k2, subln_g,
              forget_b, w_branch, w_o, ln1_g, ln1_b, w_up, conv_w, conv_b, w_down, ln2_g, ln2_b):
    for l in range(DEPTH):
        lambda_init = 0.8 - 0.6 * math.exp(-0.3 * l)
        mod = jnp.einsum('bd,de->be', jax.nn.silu(c), w_ada[l]) + b_ada[l]
        sh_a, sc_a, g_a, sh_f, sc_f, g_f = [m[:, None, :] for m in jnp.split(mod, 6, axis=-1)]
        h = x * (1.0 + sc_a) + sh_a
        mix = _mixer_sublayer(h, positions, lambda_init, w_in[l], lam_q1[l], lam_k1[l], lam_q2[l],
                              lam_k2[l], subln_g[l], forget_b[l], w_branch[l], w_o[l])
        x = _layer_norm(DEEPNORM_ALPHA * x + g_a * mix, ln1_g[l], ln1_b[l])
        h = x * (1.0 + sc_f) + sh_f
        ffn = _conv_ffn(h, w_up[l], conv_w[l], conv_b[l], w_down[l])
        x = _layer_norm(DEEPNORM_ALPHA * x + g_f * ffn, ln2_g[l], ln2_b[l])
    return x
```

```python
import functools
import math

import numpy as np
import jax
import jax.numpy as jnp
from jax import lax
from jax.experimental import pallas as pl
from jax.experimental.pallas import tpu as pltpu

D_MODEL = 1024
DEPTH = 2
HEAD_DIM = 64
DIFF_DIM = HEAD_DIM // 2
N_HEADS = 4
BRANCH_WIDTH = N_HEADS * HEAD_DIM
N_BRANCHES = 4
ROPE_THETA = 500000.0
ROPE_FRACTION = 4
DILATED_PATTERNS = ((128, 1), (512, 4), (2048, 16))
D_FF = 2048
CONV_WIDTH = 3
LN_EPS = 1e-5
DEEPNORM_ALPHA = (2.0 * DEPTH) ** 0.25

SEQ_TILE = 512
LANES = 128
F_ROWS = 16
ONES_ROWS = 16
NEG_BIG = -1e30
VMEM_LIMIT = 56 * 2**20

F32 = jnp.float32
BF16 = jnp.bfloat16
_NT = (((1,), (1,)), ((), ()))


def _dot(a, b):
    return jnp.dot(a, b, preferred_element_type=F32)


def _params(*sem):
    return pltpu.CompilerParams(dimension_semantics=sem, vmem_limit_bytes=VMEM_LIMIT)


def _split3(x):
    hi = x.astype(BF16)
    r1 = x - hi.astype(F32)
    mid = r1.astype(BF16)
    lo = (r1 - mid.astype(F32)).astype(BF16)
    return hi, mid, lo


def _layer_norm(r, g, b):
    mu = jnp.mean(r, axis=-1, keepdims=True)
    d = r - mu
    var = jnp.mean(d * d, axis=-1, keepdims=True)
    return d * lax.rsqrt(var + LN_EPS) * g + b


def _mod_kernel(c_ref, w_ref, b_ref, o_ref):
    c = c_ref[...]
    a = c / (1.0 + jnp.exp(-c))
    a_hi = a.astype(BF16)
    a_lo = (a - a_hi.astype(F32)).astype(BF16)
    w = w_ref[0]
    w_hi = w.astype(BF16)
    w_lo = (w - w_hi.astype(F32)).astype(BF16)
    o_ref[0] = _dot(a_hi, w_hi) + _dot(a_lo, w_hi) + _dot(a_hi, w_lo) + b_ref[0]


def _modulation(c, w_ada, b_ada):
    b = c.shape[0]
    rows = 8
    c_pad = jnp.zeros((rows, D_MODEL), F32).at[:b].set(c)
    out = pl.pallas_call(
        _mod_kernel,
        grid=(DEPTH, 6),
        in_specs=[
            pl.BlockSpec((rows, D_MODEL), lambda l, j: (0, 0)),
            pl.BlockSpec((1, D_MODEL, D_MODEL), lambda l, j: (l, 0, j)),
            pl.BlockSpec((1, 1, D_MODEL), lambda l, j: (l, 0, j)),
        ],
        out_specs=pl.BlockSpec((1, rows, D_MODEL), lambda l, j: (l, 0, j)),
        out_shape=jax.ShapeDtypeStruct((DEPTH, rows, 6 * D_MODEL), F32),
        compiler_params=_params("arbitrary", "arbitrary"),
        name="adaln_mod",
    )(c_pad, w_ada, b_ada.reshape(DEPTH, 1, 6 * D_MODEL))
    return out[:, :b].reshape(DEPTH, b, 6, 1, D_MODEL)


def _rope_kernel(pos_ref, invf_ref, sgn_ref, cos_ref, sin_ref):
    ang = invf_ref[...] * pos_ref[0].astype(F32)
    cos_ref[0] = jnp.cos(ang)
    sin_ref[0] = jnp.sin(ang) * sgn_ref[...]


def _rope_tables(positions):
    b, s = positions.shape
    r_a, r_c = DIFF_DIM // ROPE_FRACTION, HEAD_DIM // ROPE_FRACTION
    inv_a = jnp.power(jnp.float32(ROPE_THETA), -2.0 * jnp.arange(r_a // 2, dtype=F32) / r_a)
    inv_c = jnp.power(jnp.float32(ROPE_THETA), -2.0 * jnp.arange(r_c // 2, dtype=F32) / r_c)
    invf = jnp.concatenate([inv_a, inv_a, inv_c]).reshape(16, 1)
    sgn = jnp.concatenate([-jnp.ones(4, F32), jnp.ones(12, F32)]).reshape(16, 1)
    return pl.pallas_call(
        _rope_kernel,
        grid=(b,),
        in_specs=[
            pl.BlockSpec((1, 1, s), lambda i: (i, 0, 0)),
            pl.BlockSpec((16, 1), lambda i: (0, 0)),
            pl.BlockSpec((16, 1), lambda i: (0, 0)),
        ],
        out_specs=[pl.BlockSpec((1, 16, s), lambda i: (i, 0, 0))] * 2,
        out_shape=[jax.ShapeDtypeStruct((b, 16, s), F32)] * 2,
        compiler_params=_params("arbitrary"),
        name="rope_tables",
    )(positions.reshape(b, 1, s), invf, sgn)


def _rope_rows(y, cos, sin):
    cos_a, sin_a, cos_c, sin_c = cos[0:8], sin[0:8], cos[8:16], sin[8:16]
    parts = []
    for hh in range(2 * N_HEADS):
        r0 = DIFF_DIM * hh
        seg = y[r0:r0 + 8]
        parts.append(seg * cos_a + pltpu.roll(seg, 4, 0) * sin_a)
        parts.append(y[r0 + 8:r0 + DIFF_DIM])
    parts.append(y[BRANCH_WIDTH:2 * BRANCH_WIDTH])
    for hh in range(N_HEADS):
        r0 = 2 * BRANCH_WIDTH + HEAD_DIM * hh
        t0, t1 = y[r0:r0 + 8], y[r0 + 8:r0 + 16]
        parts.append(t0 * cos_c - t1 * sin_c)
        parts.append(t1 * cos_c + t0 * sin_c)
        parts.append(y[r0 + 16:r0 + HEAD_DIM])
    parts.append(y[3 * BRANCH_WIDTH:4 * BRANCH_WIDTH])
    return jnp.concatenate(parts, axis=0)


def _proj_kernel(x_ref, sc_ref, sh_ref, wq_ref, wk_ref, wv_ref, fb_ref, cos_ref, sin_ref, u_ref,
                 qv_ref, k_ref, f_ref, fc_ref, carry_ref):
    t = x_ref.shape[1]
    nq = 4 * BRANCH_WIDTH

    @pl.when(pl.program_id(1) == 0)
    def _():
        carry_ref[...] = jnp.zeros_like(carry_ref)

    h = (x_ref[0] * (1.0 + sc_ref[0, 0]) + sh_ref[0, 0]).astype(BF16)
    cos, sin = cos_ref[0], sin_ref[0]

    yq = _rope_rows(lax.dot_general(wq_ref[...], h, _NT, preferred_element_type=F32), cos, sin)
    qv_ref[0, 0, 0:nq, :] = yq.astype(BF16)
    yk = _rope_rows(lax.dot_general(wk_ref[...], h, _NT, preferred_element_type=F32), cos, sin)
    k_ref[0] = yk.T.astype(BF16)
    yv = lax.dot_general(wv_ref[...], h, _NT, preferred_element_type=F32)
    qv_ref[0, 0, nq:2 * nq, :] = yv[0:nq].astype(BF16)

    fl = yv[nq:nq + F_ROWS] + fb_ref[...]
    lf = jnp.minimum(fl, 0.0) - jnp.log(1.0 + jnp.exp(-jnp.abs(fl)))
    cs = _dot(jnp.concatenate(_split3(lf), axis=0), u_ref[...])
    fcum = cs[0:F_ROWS] + cs[F_ROWS:2 * F_ROWS] + cs[2 * F_ROWS:3 * F_ROWS] + carry_ref[:, 0:1]
    carry_ref[...] = jnp.broadcast_to(fcum[:, t - 1:t], carry_ref.shape)
    f_ref[0] = fcum[0:8]
    fc_ref[0] = jnp.concatenate([fcum, jnp.zeros((LANES - F_ROWS, t), F32)], axis=0).T


def _projection(x, sc, sh, wq, wk, wv, fb, cos, sin, l):
    b, s, d = x.shape
    t = SEQ_TILE
    ns = s // t
    nq = 4 * BRANCH_WIDTH
    u = jnp.asarray(np.triu(np.ones((t, t), np.float32)), BF16)
    const = lambda shape: pl.BlockSpec(shape, lambda i, j: (0,) * len(shape))
    return pl.pallas_call(
        _proj_kernel,
        grid=(b, ns),
        in_specs=[
            pl.BlockSpec((1, t, d), lambda i, j: (i, j, 0)),
            pl.BlockSpec((1, 1, 1, d), lambda i, j: (i, 1, 0, 0)),
            pl.BlockSpec((1, 1, 1, d), lambda i, j: (i, 0, 0, 0)),
            const((nq, d)), const((nq, d)), const((nq + F_ROWS, d)),
            const((F_ROWS, 1)),
            pl.BlockSpec((1, 16, t), lambda i, j: (i, 0, j)),
            pl.BlockSpec((1, 16, t), lambda i, j: (i, 0, j)),
            const((t, t)),
        ],
        out_specs=[
            pl.BlockSpec((1, 1, 2 * nq, t), lambda i, j: (i, j, 0, 0)),
            pl.BlockSpec((1, t, nq), lambda i, j: (i, j, 0)),
            pl.BlockSpec((1, 8, t), lambda i, j: (i, 0, j)),
            pl.BlockSpec((1, t, LANES), lambda i, j: (i, j, 0)),
        ],
        out_shape=[
            jax.ShapeDtypeStruct((b, ns, 2 * nq, t), BF16),
            jax.ShapeDtypeStruct((b, s, nq), BF16),
            jax.ShapeDtypeStruct((b, 8, s), F32),
            jax.ShapeDtypeStruct((b, s, LANES), F32),
        ],
        scratch_shapes=[pltpu.VMEM((F_ROWS, LANES), F32)],
        compiler_params=_params("arbitrary", "arbitrary"),
        name=f"projection_l{l}",
    )(x, sc, sh, wq, wk, wv, fb, cos, sin, u)


def _head_operands(q_ref, k_ref, v_ref, head, width):
    per_group = LANES // width
    g = head // per_group
    t = q_ref.shape[-1]
    qg = q_ref[0, 0, LANES * g:LANES * (g + 1), :]
    rows = lax.broadcasted_iota(jnp.int32, (LANES, t), 0)
    r0 = width * (head % per_group)
    qm = jnp.where((rows >= r0) & (rows < r0 + width), qg, jnp.zeros_like(qg))

    def load_k(kb):
        k0 = pl.multiple_of(kb * t, t)
        return k_ref[0, pl.ds(k0, t), LANES * g:LANES * (g + 1)]

    return qm, load_k


def _softmax_step(s, m, acc, vblk):
    t = s.shape[1]
    m_new = jnp.maximum(m, jnp.max(s, axis=0, keepdims=True))
    p = jnp.exp(s - m_new).astype(BF16)
    alpha = jnp.exp(m - m_new)
    vaug = jnp.concatenate([vblk, jnp.ones((ONES_ROWS, t), BF16)], axis=0)
    return m_new, alpha * acc + _dot(vaug, p)


def _causal_tile(t, strict):
    key = lax.broadcasted_iota(jnp.int32, (t, t), 0)
    qry = lax.broadcasted_iota(jnp.int32, (t, t), 1)
    return key < qry if strict else key <= qry


def _softmax_init(t):
    return jnp.full((1, t), NEG_BIG, F32), jnp.zeros((HEAD_DIM + ONES_ROWS, t), F32)


def _diff_kernel(q_ref, k_ref, v_ref, lam_ref, g_ref, o_ref, *, lambda_init):
    qb = pl.program_id(1)
    t = q_ref.shape[-1]
    lam_p = lam_ref[...]
    lam = (jnp.exp(jnp.sum(lam_p[0:1] * lam_p[1:2], keepdims=True))
           - jnp.exp(jnp.sum(lam_p[2:3] * lam_p[3:4], keepdims=True)) + lambda_init)
    comps = []
    for head in range(2 * N_HEADS):
        qm, load_k = _head_operands(q_ref, k_ref, v_ref, head, DIFF_DIM)
        vh = head // 2

        def step(kb, carry, diag):
            s = _dot(load_k(kb), qm)
            if diag:
                s = jnp.where(_causal_tile(t, False), s, NEG_BIG)
            return _softmax_step(s, carry[0], carry[1], v_ref[0, kb, HEAD_DIM * vh:HEAD_DIM * (vh + 1), :])

        carry = lax.fori_loop(0, qb, lambda kb, c: step(kb, c, False), _softmax_init(t))
        _, acc = step(qb, carry, True)
        comps.append(acc[0:HEAD_DIM] / acc[HEAD_DIM:HEAD_DIM + 1])
    outs = []
    for hh in range(N_HEADS):
        o = comps[2 * hh] - lam * comps[2 * hh + 1]
        o = o * lax.rsqrt(jnp.mean(o * o, axis=0, keepdims=True) + LN_EPS)
        outs.append(o * (g_ref[...] * (1.0 - lambda_init)))
    o_ref[0] = jnp.concatenate(outs, axis=0).T.astype(BF16)


def _stick_kernel(q_ref, k_ref, v_ref, ut_ref, o_ref):
    qb = pl.program_id(1)
    t = q_ref.shape[-1]
    outs = []
    for head in range(N_HEADS):
        qm, load_k = _head_operands(q_ref, k_ref, v_ref, head, HEAD_DIM)

        def step(kb, carry, diag):
            run, acc = carry
            z = _dot(load_k(kb), qm)
            lk = -(jnp.maximum(z, 0.0) + jnp.log(1.0 + jnp.exp(-jnp.abs(z))))
            lkm = jnp.where(_causal_tile(t, True), lk, 0.0) if diag else lk
            hi = lkm.astype(BF16)
            lo = (lkm - hi.astype(F32)).astype(BF16)
            after = _dot(ut_ref[...], hi) + _dot(ut_ref[...], lo)
            a = jnp.exp(z + lk + after + run)
            if diag:
                a = jnp.where(_causal_tile(t, True), a, 0.0)
            acc = acc + _dot(v_ref[0, kb, HEAD_DIM * head:HEAD_DIM * (head + 1), :], a.astype(BF16))
            return run + jnp.sum(lkm, axis=0, keepdims=True), acc

        carry = step(qb, (jnp.zeros((1, t), F32), jnp.zeros((HEAD_DIM, t), F32)), True)
        _, acc = lax.fori_loop(0, qb, lambda i, c: step(qb - 1 - i, c, False), carry)
        outs.append(acc)
    o_ref[0] = jnp.concatenate(outs, axis=0).T.astype(BF16)


def _dilated_kernel(q_ref, k_ref, v_ref, bias_ref, o_ref):
    qb = pl.program_id(1)
    t = q_ref.shape[-1]
    n_back = bias_ref.shape[0] - 1
    outs = []
    for head in range(N_HEADS):
        qm, load_k = _head_operands(q_ref, k_ref, v_ref, head, HEAD_DIM)

        def step(kb, carry):
            s = _dot(load_k(kb), qm) + bias_ref[qb - kb]
            return _softmax_step(s, carry[0], carry[1], v_ref[0, kb, HEAD_DIM * head:HEAD_DIM * (head + 1), :])

        _, acc = lax.fori_loop(jnp.maximum(qb - n_back, 0), qb + 1, step, _softmax_init(t))
        outs.append(acc[0:HEAD_DIM] / acc[HEAD_DIM:HEAD_DIM + 1])
    o_ref[0] = jnp.concatenate(outs, axis=0).T.astype(BF16)


def _forget_kernel(q_ref, k_ref, v_ref, f_ref, fc_ref, o_ref):
    qb = pl.program_id(1)
    t = q_ref.shape[-1]
    outs = []
    for head in range(N_HEADS):
        qm, load_k = _head_operands(q_ref, k_ref, v_ref, head, HEAD_DIM)
        fq = f_ref[0, head:head + 1, :]

        def step(kb, carry, diag):
            k0 = pl.multiple_of(kb * t, t)
            fk = fc_ref[0, pl.ds(k0, t), head:head + 1]
            s = _dot(load_k(kb), qm) + (fq - fk)
            if diag:
                s = jnp.where(_causal_tile(t, False), s, NEG_BIG)
            return _softmax_step(s, carry[0], carry[1], v_ref[0, kb, HEAD_DIM * head:HEAD_DIM * (head + 1), :])

        carry = lax.fori_loop(0, qb, lambda kb, c: step(kb, c, False), _softmax_init(t))
        _, acc = step(qb, carry, True)
        outs.append(acc[0:HEAD_DIM] / acc[HEAD_DIM:HEAD_DIM + 1])
    o_ref[0] = jnp.concatenate(outs, axis=0).T.astype(BF16)


def _dilated_bias(t):
    n_back = max(w for w, _ in DILATED_PATTERNS) // t
    kk = np.arange(t)[:, None]
    qq = np.arange(t)[None, :]
    tiles = []
    for d in range(n_back + 1):
        dist = d * t + qq - kk
        count = np.zeros((t, t), np.float64)
        for window, dil in DILATED_PATTERNS:
            count += (dist >= 0) & (dist % dil == 0) & (dist // dil <= window // dil)
        tiles.append(np.where(count > 0, np.log(np.maximum(count, 1.0)), NEG_BIG))
    return jnp.asarray(np.stack(tiles), F32)


def _attention(kind, branch, qv, k, extra_inputs, extra_specs, kernel, l):
    b, ns, _, t = qv.shape
    s = ns * t
    in_specs = [
        pl.BlockSpec((1, 1, BRANCH_WIDTH, t), lambda i, j: (i, j, branch, 0)),
        pl.BlockSpec((1, s, BRANCH_WIDTH), lambda i, j: (i, 0, branch)),
        pl.BlockSpec((1, ns, BRANCH_WIDTH, t), lambda i, j: (i, 0, N_BRANCHES + branch, 0)),
    ] + extra_specs
    return pl.pallas_call(
        kernel,
        grid=(b, ns),
        in_specs=in_specs,
        out_specs=pl.BlockSpec((1, t, BRANCH_WIDTH), lambda i, j: (i, j, 0)),
        out_shape=jax.ShapeDtypeStruct((b, s, BRANCH_WIDTH), BF16),
        compiler_params=_params("arbitrary", "arbitrary"),
        name=f"{kind}_attention_l{l}",
    )(qv, k, qv, *extra_inputs)


def _merge_kernel(x_ref, sc_ref, sh_ref, ga_ref, oa_ref, ob_ref, oc_ref, od_ref, wg_ref, wb_ref, wo_ref,
                  lng_ref, lnb_ref, out_ref):
    x = x_ref[0]
    d = x.shape[-1]
    h = (x * (1.0 + sc_ref[0, 0]) + sh_ref[0, 0]).astype(BF16)
    merged = None
    for n, o_ref in enumerate((oa_ref, ob_ref, oc_ref, od_ref)):
        gate = 1.0 / (1.0 + jnp.exp(-_dot(h, wg_ref[:, d * n:d * (n + 1)])))
        y = gate * _dot(o_ref[0], wb_ref[n])
        merged = y if merged is None else merged + y
    mix = _dot(merged.astype(BF16), wo_ref[...])
    out_ref[0] = _layer_norm(DEEPNORM_ALPHA * x + ga_ref[0, 0] * mix, lng_ref[...], lnb_ref[...])


def _merge(x, mod, outs, wg, wb, wo, ln_g, ln_b, l):
    b, s, d = x.shape
    t = SEQ_TILE
    const = lambda shape: pl.BlockSpec(shape, lambda i, j: (0,) * len(shape))
    mod_spec = lambda idx: pl.BlockSpec((1, 1, 1, d), lambda i, j: (i, idx, 0, 0))
    o_spec = pl.BlockSpec((1, t, BRANCH_WIDTH), lambda i, j: (i, j, 0))
    return pl.pallas_call(
        _merge_kernel,
        grid=(b, s // t),
        in_specs=[pl.BlockSpec((1, t, d), lambda i, j: (i, j, 0)), mod_spec(1), mod_spec(0), mod_spec(2),
                  o_spec, o_spec, o_spec, o_spec,
                  const((d, N_BRANCHES * d)), const((N_BRANCHES, BRANCH_WIDTH, d)), const((d, d)),
                  const((1, d)), const((1, d))],
        out_specs=pl.BlockSpec((1, t, d), lambda i, j: (i, j, 0)),
        out_shape=jax.ShapeDtypeStruct((b, s, d), F32),
        compiler_params=_params("arbitrary", "arbitrary"),
        name=f"merge_l{l}",
    )(x, mod, mod, mod, *outs, wg, wb, wo, ln_g.reshape(1, d), ln_b.reshape(1, d))


def _ffn_kernel(x_ref, sc_ref, sh_ref, gf_ref, wup_ref, cw_ref, cb_ref, wdn_ref, lng_ref, lnb_ref,
                out_ref, tail_ref):
    t = x_ref.shape[1]
    chunk = D_FF // 2

    @pl.when(pl.program_id(1) == 0)
    def _():
        tail_ref[...] = jnp.zeros_like(tail_ref)

    x = x_ref[0]
    h = (x * (1.0 + sc_ref[0, 0]) + sh_ref[0, 0]).astype(BF16)
    rows = lax.broadcasted_iota(jnp.int32, (8, chunk), 0)

    def conv(col0):
        cols = slice(col0, col0 + chunk)
        u = _dot(h, wup_ref[:, cols])
        tail = tail_ref[:, cols]
        u1, u2 = pltpu.roll(u, 1, 0), pltpu.roll(u, 2, 0)
        head1 = jnp.where(rows < 1, pltpu.roll(tail, 1, 0), u1[0:8])
        head2 = jnp.where(rows < 2, pltpu.roll(tail, 2, 0), u2[0:8])
        u1 = jnp.concatenate([head1, u1[8:]], axis=0)
        u2 = jnp.concatenate([head2, u2[8:]], axis=0)
        tail_ref[:, cols] = u[t - 8:t]
        return cw_ref[2:3, cols] * u + cw_ref[1:2, cols] * u1 + cw_ref[0:1, cols] * u2 + cb_ref[:, cols]

    ffn = None
    for c in range(D_FF // chunk):
        a = conv(c * chunk)
        g = conv(D_FF + c * chunk)
        act = (a / (1.0 + jnp.exp(-a)) * g).astype(BF16)
        y = _dot(act, wdn_ref[c * chunk:(c + 1) * chunk, :])
        ffn = y if ffn is None else ffn + y
    out_ref[0] = _layer_norm(DEEPNORM_ALPHA * x + gf_ref[0, 0] * ffn, lng_ref[...], lnb_ref[...])


def _conv_ffn(x, mod, wup, cw, cb, wdn, ln_g, ln_b, l):
    b, s, d = x.shape
    t = SEQ_TILE
    const = lambda shape: pl.BlockSpec(shape, lambda i, j: (0,) * len(shape))
    mod_spec = lambda idx: pl.BlockSpec((1, 1, 1, d), lambda i, j: (i, idx, 0, 0))
    return pl.pallas_call(
        _ffn_kernel,
        grid=(b, s // t),
        in_specs=[pl.BlockSpec((1, t, d), lambda i, j: (i, j, 0)), mod_spec(4), mod_spec(3), mod_spec(5),
                  const((d, 2 * D_FF)), const((CONV_WIDTH, 2 * D_FF)), const((1, 2 * D_FF)), const((D_FF, d)),
                  const((1, d)), const((1, d))],
        out_specs=pl.BlockSpec((1, t, d), lambda i, j: (i, j, 0)),
        out_shape=jax.ShapeDtypeStruct((b, s, d), F32),
        scratch_shapes=[pltpu.VMEM((8, 2 * D_FF), F32)],
        compiler_params=_params("arbitrary", "arbitrary"),
        name=f"conv_ffn_l{l}",
    )(x, mod, mod, mod, wup, cw, cb.reshape(1, 2 * D_FF), wdn, ln_g.reshape(1, d), ln_b.reshape(1, d))


def kernel(x, c, positions, w_ada, b_ada, w_in, lam_q1, lam_k1, lam_q2, lam_k2, subln_g, forget_b, w_branch,
           w_o, ln1_g, ln1_b, w_up, conv_w, conv_b, w_down, ln2_g, ln2_b):
    b, s, d = x.shape
    t = SEQ_TILE
    bw = BRANCH_WIDTH
    const = lambda shape: pl.BlockSpec(shape, lambda i, j: (0,) * len(shape))

    mod = _modulation(c, w_ada, b_ada)
    cos, sin = _rope_tables(positions)
    bias = _dilated_bias(t)
    ut = jnp.asarray(np.tril(np.ones((t, t), np.float32), -1).T, BF16)

    for l in range(DEPTH):
        lambda_init = 0.8 - 0.6 * math.exp(-0.3 * l)
        w = w_in[l]
        col = lambda i: w[:, bw * i:bw * (i + 1)]
        wq = jnp.concatenate([col(0) * DIFF_DIM ** -0.5, col(3) * HEAD_DIM ** -0.5, col(6) * HEAD_DIM ** -0.5,
                              col(9) * HEAD_DIM ** -0.5], axis=1).T.astype(BF16)
        wk = jnp.concatenate([col(1), col(4), col(7), col(10)], axis=1).T.astype(BF16)
        wf = jnp.zeros((d, F_ROWS), F32).at[:, :N_HEADS].set(w[:, 12 * bw:12 * bw + N_HEADS])
        wv = jnp.concatenate([col(2), col(5), col(8), col(11), wf], axis=1).T.astype(BF16)
        wg = w[:, 12 * bw + N_HEADS:].astype(BF16)
        fb = jnp.zeros((F_ROWS, 1), F32).at[:N_HEADS, 0].set(forget_b[l])

        qv, k, f_row, f_col = _projection(x, mod[l], mod[l], wq, wk, wv, fb, cos, sin, l)

        lam = jnp.stack([lam_q1[l], lam_k1[l], lam_q2[l], lam_k2[l]])
        oa = _attention("diff", 0, qv, k, [lam, subln_g[l].reshape(HEAD_DIM, 1)],
                        [const((4, DIFF_DIM)), const((HEAD_DIM, 1))],
                        functools.partial(_diff_kernel, lambda_init=lambda_init), l)
        ob = _attention("stick", 1, qv, k, [ut], [const((t, t))], _stick_kernel, l)
        oc = _attention("dilated", 2, qv, k, [bias], [const(bias.shape)], _dilated_kernel, l)
        od = _attention("forget", 3, qv, k, [f_row, f_col],
                        [pl.BlockSpec((1, 8, t), lambda i, j: (i, 0, j)),
                         pl.BlockSpec((1, s, LANES), lambda i, j: (i, 0, 0))], _forget_kernel, l)

        x = _merge(x, mod[l], (oa, ob, oc, od), wg, w_branch[l].astype(BF16), w_o[l].astype(BF16),
                   ln1_g[l], ln1_b[l], l)
        x = _conv_ffn(x, mod[l], w_up[l].astype(BF16), conv_w[l], conv_b[l], w_down[l].astype(BF16),
                      ln2_g[l], ln2_b[l], l)
    return x
```

```python
import functools
import math

import numpy as np
import jax
import jax.numpy as jnp
from jax import lax
from jax.experimental import pallas as pl
from jax.experimental.pallas import tpu as pltpu

D_MODEL = 1024
DEPTH = 2
HEAD_DIM = 64
DIFF_DIM = HEAD_DIM // 2
N_HEADS = 4
BRANCH_WIDTH = N_HEADS * HEAD_DIM
N_BRANCHES = 4
ROPE_THETA = 500000.0
ROPE_FRACTION = 4
DILATED_PATTERNS = ((128, 1), (512, 4), (2048, 16))
D_FF = 2048
CONV_WIDTH = 3
LN_EPS = 1e-5
DEEPNORM_ALPHA = (2.0 * DEPTH) ** 0.25

SEQ_TILE = 512
LANES = 128
F_ROWS = 16
ONES_ROWS = 16
NEG_BIG = -1e30
LOG2E = math.log2(math.e)
STICK_SUB = 256
AUG = LANES
VMEM_LIMIT = 56 * 2**20

F32 = jnp.float32
BF16 = jnp.bfloat16
_NT = (((1,), (1,)), ((), ()))


def _dot(a, b):
    return jnp.dot(a, b, preferred_element_type=F32)


def _params(*sem):
    return pltpu.CompilerParams(dimension_semantics=sem, vmem_limit_bytes=VMEM_LIMIT)


def _split3(x):
    hi = x.astype(BF16)
    r1 = x - hi.astype(F32)
    mid = r1.astype(BF16)
    lo = (r1 - mid.astype(F32)).astype(BF16)
    return hi, mid, lo


def _layer_norm(r, g, b):
    mu = jnp.mean(r, axis=-1, keepdims=True)
    d = r - mu
    var = jnp.mean(d * d, axis=-1, keepdims=True)
    return d * lax.rsqrt(var + LN_EPS) * g + b


def _mod_kernel(c_ref, w_ref, b_ref, o_ref):
    c = c_ref[...]
    a = c / (1.0 + jnp.exp(-c))
    a_hi = a.astype(BF16)
    a_lo = (a - a_hi.astype(F32)).astype(BF16)
    w = w_ref[0]
    w_hi = w.astype(BF16)
    w_lo = (w - w_hi.astype(F32)).astype(BF16)
    o_ref[0] = _dot(a_hi, w_hi) + _dot(a_lo, w_hi) + _dot(a_hi, w_lo) + b_ref[0]


def _modulation(c, w_ada, b_ada):
    b = c.shape[0]
    rows = 8
    c_pad = jnp.zeros((rows, D_MODEL), F32).at[:b].set(c)
    out = pl.pallas_call(
        _mod_kernel,
        grid=(DEPTH, 6),
        in_specs=[
            pl.BlockSpec((rows, D_MODEL), lambda l, j: (0, 0)),
            pl.BlockSpec((1, D_MODEL, D_MODEL), lambda l, j: (l, 0, j)),
            pl.BlockSpec((1, 1, D_MODEL), lambda l, j: (l, 0, j)),
        ],
        out_specs=pl.BlockSpec((1, rows, D_MODEL), lambda l, j: (l, 0, j)),
        out_shape=jax.ShapeDtypeStruct((DEPTH, rows, 6 * D_MODEL), F32),
        compiler_params=_params("arbitrary", "arbitrary"),
        name="adaln_mod",
    )(c_pad, w_ada, b_ada.reshape(DEPTH, 1, 6 * D_MODEL))
    return out[:, :b].reshape(DEPTH, b, 6, 1, D_MODEL)


def _rope_kernel(pos_ref, invf_ref, sgn_ref, cos_ref, sin_ref):
    ang = invf_ref[...] * pos_ref[0].astype(F32)
    cos_ref[0] = jnp.cos(ang)
    sin_ref[0] = jnp.sin(ang) * sgn_ref[...]


def _rope_tables(positions):
    b, s = positions.shape
    r_a, r_c = DIFF_DIM // ROPE_FRACTION, HEAD_DIM // ROPE_FRACTION
    inv_a = jnp.power(jnp.float32(ROPE_THETA), -2.0 * jnp.arange(r_a // 2, dtype=F32) / r_a)
    inv_c = jnp.power(jnp.float32(ROPE_THETA), -2.0 * jnp.arange(r_c // 2, dtype=F32) / r_c)
    invf = jnp.concatenate([inv_a, inv_a, inv_c]).reshape(16, 1)
    sgn = jnp.concatenate([-jnp.ones(4, F32), jnp.ones(12, F32)]).reshape(16, 1)
    return pl.pallas_call(
        _rope_kernel,
        grid=(b,),
        in_specs=[
            pl.BlockSpec((1, 1, s), lambda i: (i, 0, 0)),
            pl.BlockSpec((16, 1), lambda i: (0, 0)),
            pl.BlockSpec((16, 1), lambda i: (0, 0)),
        ],
        out_specs=[pl.BlockSpec((1, 16, s), lambda i: (i, 0, 0))] * 2,
        out_shape=[jax.ShapeDtypeStruct((b, 16, s), F32)] * 2,
        compiler_params=_params("arbitrary"),
        name="rope_tables",
    )(positions.reshape(b, 1, s), invf, sgn)


def _rope_rows(y, cos, sin):
    cos_a, sin_a, cos_c, sin_c = cos[0:8], sin[0:8], cos[8:16], sin[8:16]
    parts = []
    for hh in range(2 * N_HEADS):
        r0 = DIFF_DIM * hh
        seg = y[r0:r0 + 8]
        parts.append(seg * cos_a + pltpu.roll(seg, 4, 0) * sin_a)
        parts.append(y[r0 + 8:r0 + DIFF_DIM])
    parts.append(y[BRANCH_WIDTH:2 * BRANCH_WIDTH])
    for hh in range(N_HEADS):
        r0 = 2 * BRANCH_WIDTH + HEAD_DIM * hh
        t0, t1 = y[r0:r0 + 8], y[r0 + 8:r0 + 16]
        parts.append(t0 * cos_c - t1 * sin_c)
        parts.append(t1 * cos_c + t0 * sin_c)
        parts.append(y[r0 + 16:r0 + HEAD_DIM])
    parts.append(y[3 * BRANCH_WIDTH:4 * BRANCH_WIDTH])
    return jnp.concatenate(parts, axis=0)


def _proj_kernel(x_ref, sc_ref, sh_ref, wq_ref, wk_ref, wv_ref, fb_ref, cos_ref, sin_ref, u_ref,
                 qv_ref, k_ref, carry_ref):
    t = x_ref.shape[1]
    nq = 4 * BRANCH_WIDTH
    n3 = 3 * BRANCH_WIDTH

    @pl.when(pl.program_id(1) == 0)
    def _():
        carry_ref[...] = jnp.zeros_like(carry_ref)

    h = (x_ref[0] * (1.0 + sc_ref[0, 0]) + sh_ref[0, 0]).astype(BF16)
    cos, sin = cos_ref[0], sin_ref[0]

    yq = _rope_rows(lax.dot_general(wq_ref[...], h, _NT, preferred_element_type=F32), cos, sin)
    yk = _rope_rows(lax.dot_general(wk_ref[...], h, _NT, preferred_element_type=F32), cos, sin)
    yv = lax.dot_general(wv_ref[...], h, _NT, preferred_element_type=F32)

    fl = yv[nq:nq + F_ROWS] + fb_ref[...]
    lf = jnp.minimum(fl, 0.0) - jnp.log(1.0 + jnp.exp(-jnp.abs(fl)))
    cs = _dot(jnp.concatenate(_split3(lf), axis=0), u_ref[...])
    fcum = cs[0:F_ROWS] + cs[F_ROWS:2 * F_ROWS] + cs[2 * F_ROWS:3 * F_ROWS] + carry_ref[:, 0:1]
    carry_ref[...] = jnp.broadcast_to(fcum[:, t - 1:t], carry_ref.shape)

    f2 = fcum * LOG2E
    row8 = lax.broadcasted_iota(jnp.int32, (8, t), 0)
    ones3 = jnp.where(row8 < 3, 1.0, 0.0)
    pad = jnp.zeros((AUG - HEAD_DIM - 16, t), F32)
    q_rows, k_rows = [], []
    for hh in range(N_HEADS):
        hi, mid, lo = (p.astype(F32) for p in _split3(f2[hh:hh + 1]))
        pieces = jnp.where(row8 == 0, hi, jnp.where(row8 == 1, mid, jnp.where(row8 == 2, lo, 0.0)))
        feat = slice(n3 + HEAD_DIM * hh, n3 + HEAD_DIM * (hh + 1))
        q_rows += [yq[feat], ones3, pieces, pad]
        k_rows += [yk[feat], -pieces, ones3, pad]
    qv_ref[0, 0, 0:N_HEADS * AUG + n3, :] = jnp.concatenate(q_rows + [yq[0:n3]], axis=0).astype(BF16)
    qv_ref[0, 0, N_HEADS * AUG + n3:N_HEADS * AUG + n3 + nq, :] = yv[0:nq].astype(BF16)
    k_ref[0] = jnp.concatenate(k_rows + [yk[0:n3]], axis=0).T.astype(BF16)


def _projection(x, sc, sh, wq, wk, wv, fb, cos, sin, l):
    b, s, d = x.shape
    t = SEQ_TILE
    ns = s // t
    nq = 4 * BRANCH_WIDTH
    nk = N_HEADS * AUG + 3 * BRANCH_WIDTH
    u = jnp.asarray(np.triu(np.ones((t, t), np.float32)), BF16)
    const = lambda shape: pl.BlockSpec(shape, lambda i, j: (0,) * len(shape))
    return pl.pallas_call(
        _proj_kernel,
        grid=(b, ns),
        in_specs=[
            pl.BlockSpec((1, t, d), lambda i, j: (i, j, 0)),
            pl.BlockSpec((1, 1, 1, d), lambda i, j: (i, 1, 0, 0)),
            pl.BlockSpec((1, 1, 1, d), lambda i, j: (i, 0, 0, 0)),
            const((nq, d)), const((nq, d)), const((nq + F_ROWS, d)),
            const((F_ROWS, 1)),
            pl.BlockSpec((1, 16, t), lambda i, j: (i, 0, j)),
            pl.BlockSpec((1, 16, t), lambda i, j: (i, 0, j)),
            const((t, t)),
        ],
        out_specs=[
            pl.BlockSpec((1, 1, nk + nq, t), lambda i, j: (i, j, 0, 0)),
            pl.BlockSpec((1, t, nk), lambda i, j: (i, j, 0)),
        ],
        out_shape=[
            jax.ShapeDtypeStruct((b, ns, nk + nq, t), BF16),
            jax.ShapeDtypeStruct((b, s, nk), BF16),
        ],
        scratch_shapes=[pltpu.VMEM((F_ROWS, LANES), F32)],
        compiler_params=_params("arbitrary", "arbitrary"),
        name=f"projection_l{l}",
    )(x, sc, sh, wq, wk, wv, fb, cos, sin, u)


def _masked_q(q_ref, head, width):
    per_group = LANES // width
    g = head // per_group
    qg = q_ref[0, 0, LANES * g:LANES * (g + 1), :]
    rows = lax.broadcasted_iota(jnp.int32, qg.shape, 0)
    r0 = width * (head % per_group)
    return jnp.where((rows >= r0) & (rows < r0 + width), qg, jnp.zeros_like(qg))


def _load_k(k_ref, kb, group, t):
    k0 = pl.multiple_of(kb * t, t)
    return k_ref[0, pl.ds(k0, t), LANES * group:LANES * (group + 1)]


def _load_v(v_ref, kb, head):
    return v_ref[0, kb, HEAD_DIM * head:HEAD_DIM * (head + 1), :]


def _softmax_step(s, carry, vblk):
    m, acc = carry
    t = s.shape[1]
    m_new = jnp.maximum(m, jnp.max(s, axis=0, keepdims=True))
    p = jnp.exp2(s - m_new).astype(BF16)
    alpha = jnp.exp2(m - m_new)
    vaug = jnp.concatenate([vblk, jnp.ones((ONES_ROWS, t), BF16)], axis=0)
    return m_new, alpha * acc + _dot(vaug, p)


def _causal_tile(t, strict):
    key = lax.broadcasted_iota(jnp.int32, (t, t), 0)
    qry = lax.broadcasted_iota(jnp.int32, (t, t), 1)
    return key < qry if strict else key <= qry


def _softmax_init(t):
    return jnp.full((1, t), NEG_BIG, F32), jnp.zeros((HEAD_DIM + ONES_ROWS, t), F32)


def _normalised(carry):
    acc = carry[1]
    return acc[0:HEAD_DIM] / acc[HEAD_DIM:HEAD_DIM + 1]


def _staggered(n, score_fn, step_fn):
    out = []
    s_next = score_fn(0)
    for h in range(n):
        s = s_next
        if h + 1 < n:
            s_next = score_fn(h + 1)
        out.append(step_fn(h, s))
    return tuple(out)


def _diff_kernel(q_ref, k_ref, v_ref, lam_ref, g_ref, o_ref, *, lambda_init):
    qb = pl.program_id(1)
    t = q_ref.shape[-1]
    n = 2 * N_HEADS
    per_group = LANES // DIFF_DIM
    lam_p = lam_ref[...]
    lam = (jnp.exp(jnp.sum(lam_p[0:1] * lam_p[1:2], keepdims=True))
           - jnp.exp(jnp.sum(lam_p[2:3] * lam_p[3:4], keepdims=True)) + lambda_init)
    qms = [_masked_q(q_ref, head, DIFF_DIM) for head in range(n)]

    def body(kb, carries, diag):
        kblks = [_load_k(k_ref, kb, g, t) for g in range(n // per_group)]

        def step(head, s):
            if diag:
                s = jnp.where(_causal_tile(t, False), s, NEG_BIG)
            return _softmax_step(s, carries[head], _load_v(v_ref, kb, head // 2))

        return _staggered(n, lambda head: _dot(kblks[head // per_group], qms[head]), step)

    carries = lax.fori_loop(0, qb, lambda kb, c: body(kb, c, False), tuple(_softmax_init(t) for _ in range(n)))
    comps = [_normalised(c) for c in body(qb, carries, True)]
    outs = []
    for hh in range(N_HEADS):
        o = comps[2 * hh] - lam * comps[2 * hh + 1]
        o = o * lax.rsqrt(jnp.mean(o * o, axis=0, keepdims=True) + LN_EPS)
        outs.append(o * (g_ref[...] * (1.0 - lambda_init)))
    o_ref[0] = jnp.concatenate(outs, axis=0).T.astype(BF16)


def _stick_step(z, carry, vblk, ut, diag):
    run, acc = carry
    t = z.shape[0]
    sub = ut.shape[0]
    lk = -(jnp.maximum(z, 0.0) + jnp.log(1.0 + jnp.exp(-jnp.abs(z))))
    lkm = jnp.where(_causal_tile(t, True), lk, 0.0) if diag else lk
    hi = lkm.astype(BF16)
    lo = (lkm - hi.astype(F32)).astype(BF16)
    later = jnp.zeros((1, t), F32)
    parts = []
    for i in reversed(range(t // sub)):
        rows = slice(i * sub, (i + 1) * sub)
        within = _dot(ut, hi[rows]) + _dot(ut, lo[rows])
        parts.append(within + later)
        later = later + within[0:1] + lkm[i * sub:i * sub + 1]
    after = jnp.concatenate(parts[::-1], axis=0)
    a = jnp.exp(z + lk + after + run)
    if diag:
        a = jnp.where(_causal_tile(t, True), a, 0.0)
    return run + later, acc + _dot(vblk, a.astype(BF16))


def _stick_kernel(q_ref, k_ref, v_ref, ut_ref, o_ref):
    qb = pl.program_id(1)
    t = q_ref.shape[-1]
    per_group = LANES // HEAD_DIM
    qms = [_masked_q(q_ref, head, HEAD_DIM) for head in range(N_HEADS)]

    def body(kb, carries, diag):
        kblks = [_load_k(k_ref, kb, g, t) for g in range(N_HEADS // per_group)]
        return _staggered(N_HEADS, lambda head: _dot(kblks[head // per_group], qms[head]),
                          lambda head, z: _stick_step(z, carries[head], _load_v(v_ref, kb, head), ut_ref[...], diag))

    init = tuple((jnp.zeros((1, t), F32), jnp.zeros((HEAD_DIM, t), F32)) for _ in range(N_HEADS))
    carries = lax.fori_loop(0, qb, lambda i, c: body(qb - 1 - i, c, False), body(qb, init, True))
    o_ref[0] = jnp.concatenate([c[1] for c in carries], axis=0).T.astype(BF16)


def _dilated_kernel(q_ref, k_ref, v_ref, bias_ref, o_ref):
    qb = pl.program_id(1)
    t = q_ref.shape[-1]
    n_back = bias_ref.shape[0] - 1
    per_group = LANES // HEAD_DIM
    qms = [_masked_q(q_ref, head, HEAD_DIM) for head in range(N_HEADS)]

    def body(kb, carries):
        kblks = [_load_k(k_ref, kb, g, t) for g in range(N_HEADS // per_group)]
        bias = bias_ref[qb - kb]
        return _staggered(N_HEADS, lambda head: _dot(kblks[head // per_group], qms[head]),
                          lambda head, s: _softmax_step(bias + s, carries[head], _load_v(v_ref, kb, head)))

    carries = lax.fori_loop(jnp.maximum(qb - n_back, 0), qb + 1, body,
                            tuple(_softmax_init(t) for _ in range(N_HEADS)))
    o_ref[0] = jnp.concatenate([_normalised(c) for c in carries], axis=0).T.astype(BF16)


def _forget_kernel(q_ref, k_ref, v_ref, o_ref):
    qb = pl.program_id(1)
    t = q_ref.shape[-1]
    qas = [q_ref[0, 0, AUG * head:AUG * (head + 1), :] for head in range(N_HEADS)]

    def body(kb, carries, diag):
        def step(head, s):
            if diag:
                s = jnp.where(_causal_tile(t, False), s, NEG_BIG)
            return _softmax_step(s, carries[head], _load_v(v_ref, kb, head))

        return _staggered(N_HEADS, lambda head: _dot(_load_k(k_ref, kb, head, t), qas[head]), step)

    carries = lax.fori_loop(0, qb, lambda kb, c: body(kb, c, False),
                            tuple(_softmax_init(t) for _ in range(N_HEADS)))
    o_ref[0] = jnp.concatenate([_normalised(c) for c in body(qb, carries, True)], axis=0).T.astype(BF16)


def _dilated_bias(t):
    n_back = max(w for w, _ in DILATED_PATTERNS) // t
    kk = np.arange(t)[:, None]
    qq = np.arange(t)[None, :]
    tiles = []
    for d in range(n_back + 1):
        dist = d * t + qq - kk
        count = np.zeros((t, t), np.float64)
        for window, dil in DILATED_PATTERNS:
            count += (dist >= 0) & (dist % dil == 0) & (dist // dil <= window // dil)
        tiles.append(np.where(count > 0, np.log2(np.maximum(count, 1.0)), NEG_BIG))
    return jnp.asarray(np.stack(tiles), F32)


def _attention(kind, branch, qv, k, extra_inputs, extra_specs, kernel, l):
    b, ns, _, t = qv.shape
    s = ns * t
    n_aug = N_HEADS * AUG
    if branch == 3:
        width, qk_block = n_aug, 0
    else:
        width, qk_block = BRANCH_WIDTH, n_aug // BRANCH_WIDTH + branch
    v_block = (n_aug + 3 * BRANCH_WIDTH) // BRANCH_WIDTH + branch
    in_specs = [
        pl.BlockSpec((1, 1, width, t), lambda i, j: (i, j, qk_block, 0)),
        pl.BlockSpec((1, s, width), lambda i, j: (i, 0, qk_block)),
        pl.BlockSpec((1, ns, BRANCH_WIDTH, t), lambda i, j: (i, 0, v_block, 0)),
    ] + extra_specs
    return pl.pallas_call(
        kernel,
        grid=(b, ns),
        in_specs=in_specs,
        out_specs=pl.BlockSpec((1, t, BRANCH_WIDTH), lambda i, j: (i, j, 0)),
        out_shape=jax.ShapeDtypeStruct((b, s, BRANCH_WIDTH), BF16),
        compiler_params=_params("arbitrary", "arbitrary"),
        name=f"{kind}_attention_l{l}",
    )(qv, k, qv, *extra_inputs)


def _merge_kernel(x_ref, sc_ref, sh_ref, ga_ref, oa_ref, ob_ref, oc_ref, od_ref, wg_ref, wb_ref, wo_ref,
                  lng_ref, lnb_ref, out_ref):
    x = x_ref[0]
    d = x.shape[-1]
    h = (x * (1.0 + sc_ref[0, 0]) + sh_ref[0, 0]).astype(BF16)
    merged = None
    for n, o_ref in enumerate((oa_ref, ob_ref, oc_ref, od_ref)):
        gate = 1.0 / (1.0 + jnp.exp(-_dot(h, wg_ref[:, d * n:d * (n + 1)])))
        y = gate * _dot(o_ref[0], wb_ref[n])
        merged = y if merged is None else merged + y
    mix = _dot(merged.astype(BF16), wo_ref[...])
    out_ref[0] = _layer_norm(DEEPNORM_ALPHA * x + ga_ref[0, 0] * mix, lng_ref[...], lnb_ref[...])


def _merge(x, mod, outs, wg, wb, wo, ln_g, ln_b, l):
    b, s, d = x.shape
    t = SEQ_TILE
    const = lambda shape: pl.BlockSpec(shape, lambda i, j: (0,) * len(shape))
    mod_spec = lambda idx: pl.BlockSpec((1, 1, 1, d), lambda i, j: (i, idx, 0, 0))
    o_spec = pl.BlockSpec((1, t, BRANCH_WIDTH), lambda i, j: (i, j, 0))
    return pl.pallas_call(
        _merge_kernel,
        grid=(b, s // t),
        in_specs=[pl.BlockSpec((1, t, d), lambda i, j: (i, j, 0)), mod_spec(1), mod_spec(0), mod_spec(2),
                  o_spec, o_spec, o_spec, o_spec,
                  const((d, N_BRANCHES * d)), const((N_BRANCHES, BRANCH_WIDTH, d)), const((d, d)),
                  const((1, d)), const((1, d))],
        out_specs=pl.BlockSpec((1, t, d), lambda i, j: (i, j, 0)),
        out_shape=jax.ShapeDtypeStruct((b, s, d), F32),
        compiler_params=_params("arbitrary", "arbitrary"),
        name=f"merge_l{l}",
    )(x, mod, mod, mod, *outs, wg, wb, wo, ln_g.reshape(1, d), ln_b.reshape(1, d))


def _ffn_kernel(x_ref, sc_ref, sh_ref, gf_ref, wup_ref, cw_ref, cb_ref, wdn_ref, lng_ref, lnb_ref,
                out_ref, tail_ref):
    t = x_ref.shape[1]
    chunk = D_FF // 2

    @pl.when(pl.program_id(1) == 0)
    def _():
        tail_ref[...] = jnp.zeros_like(tail_ref)

    x = x_ref[0]
    h = (x * (1.0 + sc_ref[0, 0]) + sh_ref[0, 0]).astype(BF16)
    rows = lax.broadcasted_iota(jnp.int32, (8, chunk), 0)

    def conv(col0):
        cols = slice(col0, col0 + chunk)
        u = _dot(h, wup_ref[:, cols])
        tail = tail_ref[:, cols]
        u1, u2 = pltpu.roll(u, 1, 0), pltpu.roll(u, 2, 0)
        head1 = jnp.where(rows < 1, pltpu.roll(tail, 1, 0), u1[0:8])
        head2 = jnp.where(rows < 2, pltpu.roll(tail, 2, 0), u2[0:8])
        u1 = jnp.concatenate([head1, u1[8:]], axis=0)
        u2 = jnp.concatenate([head2, u2[8:]], axis=0)
        tail_ref[:, cols] = u[t - 8:t]
        return cw_ref[2:3, cols] * u + cw_ref[1:2, cols] * u1 + cw_ref[0:1, cols] * u2 + cb_ref[:, cols]

    ffn = None
    for c in range(D_FF // chunk):
        a = conv(c * chunk)
        g = conv(D_FF + c * chunk)
        act = (a / (1.0 + jnp.exp(-a)) * g).astype(BF16)
        y = _dot(act, wdn_ref[c * chunk:(c + 1) * chunk, :])
        ffn = y if ffn is None else ffn + y
    out_ref[0] = _layer_norm(DEEPNORM_ALPHA * x + gf_ref[0, 0] * ffn, lng_ref[...], lnb_ref[...])


def _conv_ffn(x, mod, wup, cw, cb, wdn, ln_g, ln_b, l):
    b, s, d = x.shape
    t = SEQ_TILE
    const = lambda shape: pl.BlockSpec(shape, lambda i, j: (0,) * len(shape))
    mod_spec = lambda idx: pl.BlockSpec((1, 1, 1, d), lambda i, j: (i, idx, 0, 0))
    return pl.pallas_call(
        _ffn_kernel,
        grid=(b, s // t),
        in_specs=[pl.BlockSpec((1, t, d), lambda i, j: (i, j, 0)), mod_spec(4), mod_spec(3), mod_spec(5),
                  const((d, 2 * D_FF)), const((CONV_WIDTH, 2 * D_FF)), const((1, 2 * D_FF)), const((D_FF, d)),
                  const((1, d)), const((1, d))],
        out_specs=pl.BlockSpec((1, t, d), lambda i, j: (i, j, 0)),
        out_shape=jax.ShapeDtypeStruct((b, s, d), F32),
        scratch_shapes=[pltpu.VMEM((8, 2 * D_FF), F32)],
        compiler_params=_params("arbitrary", "arbitrary"),
        name=f"conv_ffn_l{l}",
    )(x, mod, mod, mod, wup, cw, cb.reshape(1, 2 * D_FF), wdn, ln_g.reshape(1, d), ln_b.reshape(1, d))


def kernel(x, c, positions, w_ada, b_ada, w_in, lam_q1, lam_k1, lam_q2, lam_k2, subln_g, forget_b, w_branch,
           w_o, ln1_g, ln1_b, w_up, conv_w, conv_b, w_down, ln2_g, ln2_b):
    b, s, d = x.shape
    t = SEQ_TILE
    bw = BRANCH_WIDTH
    const = lambda shape: pl.BlockSpec(shape, lambda i, j: (0,) * len(shape))

    mod = _modulation(c, w_ada, b_ada)
    cos, sin = _rope_tables(positions)
    bias = _dilated_bias(t)
    sub = min(STICK_SUB, t)
    ut = jnp.asarray(np.triu(np.ones((sub, sub), np.float32), 1), BF16)

    for l in range(DEPTH):
        lambda_init = 0.8 - 0.6 * math.exp(-0.3 * l)
        w = w_in[l]
        col = lambda i: w[:, bw * i:bw * (i + 1)]
        wq = jnp.concatenate([col(0) * (DIFF_DIM ** -0.5 * LOG2E), col(3) * HEAD_DIM ** -0.5,
                              col(6) * (HEAD_DIM ** -0.5 * LOG2E), col(9) * (HEAD_DIM ** -0.5 * LOG2E)],
                             axis=1).T.astype(BF16)
        wk = jnp.concatenate([col(1), col(4), col(7), col(10)], axis=1).T.astype(BF16)
        wf = jnp.zeros((d, F_ROWS), F32).at[:, :N_HEADS].set(w[:, 12 * bw:12 * bw + N_HEADS])
        wv = jnp.concatenate([col(2), col(5), col(8), col(11), wf], axis=1).T.astype(BF16)
        wg = w[:, 12 * bw + N_HEADS:].astype(BF16)
        fb = jnp.zeros((F_ROWS, 1), F32).at[:N_HEADS, 0].set(forget_b[l])

        qv, k = _projection(x, mod[l], mod[l], wq, wk, wv, fb, cos, sin, l)

        lam = jnp.stack([lam_q1[l], lam_k1[l], lam_q2[l], lam_k2[l]])
        oa = _attention("diff", 0, qv, k, [lam, subln_g[l].reshape(HEAD_DIM, 1)],
                        [const((4, DIFF_DIM)), const((HEAD_DIM, 1))],
                        functools.partial(_diff_kernel, lambda_init=lambda_init), l)
        ob = _attention("stick", 1, qv, k, [ut], [const((sub, sub))], _stick_kernel, l)
        oc = _attention("dilated", 2, qv, k, [bias], [const(bias.shape)], _dilated_kernel, l)
        od = _attention("forget", 3, qv, k, [], [], _forget_kernel, l)

        x = _merge(x, mod[l], (oa, ob, oc, od), wg, w_branch[l].astype(BF16), w_o[l].astype(BF16),
                   ln1_g[l], ln1_b[l], l)
        x = _conv_ffn(x, mod[l], w_up[l].astype(BF16), conv_w[l], conv_b[l], w_down[l].astype(BF16),
                      ln2_g[l], ln2_b[l], l)
    return x
```

```python
import functools
import math

import numpy as np
import jax
import jax.numpy as jnp
from jax import lax
from jax.experimental import pallas as pl
from jax.experimental.pallas import tpu as pltpu

D_MODEL = 1024
DEPTH = 2
HEAD_DIM = 64
DIFF_DIM = HEAD_DIM // 2
N_HEADS = 4
BRANCH_WIDTH = N_HEADS * HEAD_DIM
N_BRANCHES = 4
ROPE_THETA = 500000.0
ROPE_FRACTION = 4
DILATED_PATTERNS = ((128, 1), (512, 4), (2048, 16))
D_FF = 2048
CONV_WIDTH = 3
LN_EPS = 1e-5
DEEPNORM_ALPHA = (2.0 * DEPTH) ** 0.25

SEQ_TILE = 512
LANES = 128
F_ROWS = 16
ONES_ROWS = 16
NEG_BIG = -1e30
LOG2E = math.log2(math.e)
STICK_EXIT = -105.0
STICK_SUB = 256
AUG = LANES
VMEM_LIMIT = 56 * 2**20

F32 = jnp.float32
BF16 = jnp.bfloat16
_NT = (((1,), (1,)), ((), ()))


def _dot(a, b):
    return jnp.dot(a, b, preferred_element_type=F32)


def _params(*sem):
    return pltpu.CompilerParams(dimension_semantics=sem, vmem_limit_bytes=VMEM_LIMIT)


def _split3(x):
    hi = x.astype(BF16)
    r1 = x - hi.astype(F32)
    mid = r1.astype(BF16)
    lo = (r1 - mid.astype(F32)).astype(BF16)
    return hi, mid, lo


def _layer_norm(r, g, b):
    mu = jnp.mean(r, axis=-1, keepdims=True)
    d = r - mu
    var = jnp.mean(d * d, axis=-1, keepdims=True)
    return d * lax.rsqrt(var + LN_EPS) * g + b


def _mod_kernel(c_ref, w_ref, b_ref, o_ref):
    c = c_ref[...]
    a = c / (1.0 + jnp.exp(-c))
    a_hi = a.astype(BF16)
    a_lo = (a - a_hi.astype(F32)).astype(BF16)
    w = w_ref[0]
    w_hi = w.astype(BF16)
    w_lo = (w - w_hi.astype(F32)).astype(BF16)
    o_ref[0] = _dot(a_hi, w_hi) + _dot(a_lo, w_hi) + _dot(a_hi, w_lo) + b_ref[0]


def _modulation(c, w_ada, b_ada):
    b = c.shape[0]
    rows = 8
    c_pad = jnp.zeros((rows, D_MODEL), F32).at[:b].set(c)
    out = pl.pallas_call(
        _mod_kernel,
        grid=(DEPTH, 6),
        in_specs=[
            pl.BlockSpec((rows, D_MODEL), lambda l, j: (0, 0)),
            pl.BlockSpec((1, D_MODEL, D_MODEL), lambda l, j: (l, 0, j)),
            pl.BlockSpec((1, 1, D_MODEL), lambda l, j: (l, 0, j)),
        ],
        out_specs=pl.BlockSpec((1, rows, D_MODEL), lambda l, j: (l, 0, j)),
        out_shape=jax.ShapeDtypeStruct((DEPTH, rows, 6 * D_MODEL), F32),
        compiler_params=_params("arbitrary", "arbitrary"),
        name="adaln_mod",
    )(c_pad, w_ada, b_ada.reshape(DEPTH, 1, 6 * D_MODEL))
    return out[:, :b].reshape(DEPTH, b, 6, 1, D_MODEL)


def _rope_kernel(pos_ref, invf_ref, sgn_ref, cos_ref, sin_ref):
    ang = invf_ref[...] * pos_ref[0].astype(F32)
    cos_ref[0] = jnp.cos(ang)
    sin_ref[0] = jnp.sin(ang) * sgn_ref[...]


def _rope_tables(positions):
    b, s = positions.shape
    r_a, r_c = DIFF_DIM // ROPE_FRACTION, HEAD_DIM // ROPE_FRACTION
    inv_a = jnp.power(jnp.float32(ROPE_THETA), -2.0 * jnp.arange(r_a // 2, dtype=F32) / r_a)
    inv_c = jnp.power(jnp.float32(ROPE_THETA), -2.0 * jnp.arange(r_c // 2, dtype=F32) / r_c)
    invf = jnp.concatenate([inv_a, inv_a, inv_c]).reshape(16, 1)
    sgn = jnp.concatenate([-jnp.ones(4, F32), jnp.ones(12, F32)]).reshape(16, 1)
    return pl.pallas_call(
        _rope_kernel,
        grid=(b,),
        in_specs=[
            pl.BlockSpec((1, 1, s), lambda i: (i, 0, 0)),
            pl.BlockSpec((16, 1), lambda i: (0, 0)),
            pl.BlockSpec((16, 1), lambda i: (0, 0)),
        ],
        out_specs=[pl.BlockSpec((1, 16, s), lambda i: (i, 0, 0))] * 2,
        out_shape=[jax.ShapeDtypeStruct((b, 16, s), F32)] * 2,
        compiler_params=_params("arbitrary"),
        name="rope_tables",
    )(positions.reshape(b, 1, s), invf, sgn)


def _rope_rows(y, cos, sin):
    cos_a, sin_a, cos_c, sin_c = cos[0:8], sin[0:8], cos[8:16], sin[8:16]
    parts = []
    for hh in range(2 * N_HEADS):
        r0 = DIFF_DIM * hh
        seg = y[r0:r0 + 8]
        parts.append(seg * cos_a + pltpu.roll(seg, 4, 0) * sin_a)
        parts.append(y[r0 + 8:r0 + DIFF_DIM])
    parts.append(y[BRANCH_WIDTH:2 * BRANCH_WIDTH])
    for hh in range(N_HEADS):
        r0 = 2 * BRANCH_WIDTH + HEAD_DIM * hh
        t0, t1 = y[r0:r0 + 8], y[r0 + 8:r0 + 16]
        parts.append(t0 * cos_c - t1 * sin_c)
        parts.append(t1 * cos_c + t0 * sin_c)
        parts.append(y[r0 + 16:r0 + HEAD_DIM])
    parts.append(y[3 * BRANCH_WIDTH:4 * BRANCH_WIDTH])
    return jnp.concatenate(parts, axis=0)


def _proj_kernel(x_ref, sc_ref, sh_ref, wq_ref, wk_ref, wv_ref, fb_ref, cos_ref, sin_ref, u_ref,
                 qv_ref, k_ref, carry_ref):
    t = x_ref.shape[1]
    nq = 4 * BRANCH_WIDTH
    n3 = 3 * BRANCH_WIDTH

    @pl.when(pl.program_id(1) == 0)
    def _():
        carry_ref[...] = jnp.zeros_like(carry_ref)

    h = (x_ref[0] * (1.0 + sc_ref[0, 0]) + sh_ref[0, 0]).astype(BF16)
    cos, sin = cos_ref[0], sin_ref[0]

    yq = _rope_rows(lax.dot_general(wq_ref[...], h, _NT, preferred_element_type=F32), cos, sin)
    yk = _rope_rows(lax.dot_general(wk_ref[...], h, _NT, preferred_element_type=F32), cos, sin)
    yv = lax.dot_general(wv_ref[...], h, _NT, preferred_element_type=F32)

    fl = yv[nq:nq + F_ROWS] + fb_ref[...]
    lf = jnp.minimum(fl, 0.0) - jnp.log(1.0 + jnp.exp(-jnp.abs(fl)))
    cs = _dot(jnp.concatenate(_split3(lf), axis=0), u_ref[...])
    fcum = cs[0:F_ROWS] + cs[F_ROWS:2 * F_ROWS] + cs[2 * F_ROWS:3 * F_ROWS] + carry_ref[:, 0:1]
    carry_ref[...] = jnp.broadcast_to(fcum[:, t - 1:t], carry_ref.shape)

    f2 = fcum * LOG2E
    row8 = lax.broadcasted_iota(jnp.int32, (8, t), 0)
    ones3 = jnp.where(row8 < 3, 1.0, 0.0)
    pad = jnp.zeros((AUG - HEAD_DIM - 16, t), F32)
    q_rows, k_rows = [], []
    for hh in range(N_HEADS):
        hi, mid, lo = (p.astype(F32) for p in _split3(f2[hh:hh + 1]))
        pieces = jnp.where(row8 == 0, hi, jnp.where(row8 == 1, mid, jnp.where(row8 == 2, lo, 0.0)))
        feat = slice(n3 + HEAD_DIM * hh, n3 + HEAD_DIM * (hh + 1))
        q_rows += [yq[feat], ones3, pieces, pad]
        k_rows += [yk[feat], -pieces, ones3, pad]
    qv_ref[0, 0, 0:N_HEADS * AUG + n3, :] = jnp.concatenate(q_rows + [yq[0:n3]], axis=0).astype(BF16)
    qv_ref[0, 0, N_HEADS * AUG + n3:N_HEADS * AUG + n3 + nq, :] = yv[0:nq].astype(BF16)
    k_ref[0] = jnp.concatenate(k_rows + [yk[0:n3]], axis=0).T.astype(BF16)


def _projection(x, sc, sh, wq, wk, wv, fb, cos, sin, l):
    b, s, d = x.shape
    t = SEQ_TILE
    ns = s // t
    nq = 4 * BRANCH_WIDTH
    nk = N_HEADS * AUG + 3 * BRANCH_WIDTH
    u = jnp.asarray(np.triu(np.ones((t, t), np.float32)), BF16)
    const = lambda shape: pl.BlockSpec(shape, lambda i, j: (0,) * len(shape))
    return pl.pallas_call(
        _proj_kernel,
        grid=(b, ns),
        in_specs=[
            pl.BlockSpec((1, t, d), lambda i, j: (i, j, 0)),
            pl.BlockSpec((1, 1, 1, d), lambda i, j: (i, 1, 0, 0)),
            pl.BlockSpec((1, 1, 1, d), lambda i, j: (i, 0, 0, 0)),
            const((nq, d)), const((nq, d)), const((nq + F_ROWS, d)),
            const((F_ROWS, 1)),
            pl.BlockSpec((1, 16, t), lambda i, j: (i, 0, j)),
            pl.BlockSpec((1, 16, t), lambda i, j: (i, 0, j)),
            const((t, t)),
        ],
        out_specs=[
            pl.BlockSpec((1, 1, nk + nq, t), lambda i, j: (i, j, 0, 0)),
            pl.BlockSpec((1, t, nk), lambda i, j: (i, j, 0)),
        ],
        out_shape=[
            jax.ShapeDtypeStruct((b, ns, nk + nq, t), BF16),
            jax.ShapeDtypeStruct((b, s, nk), BF16),
        ],
        scratch_shapes=[pltpu.VMEM((F_ROWS, LANES), F32)],
        compiler_params=_params("arbitrary", "arbitrary"),
        name=f"projection_l{l}",
    )(x, sc, sh, wq, wk, wv, fb, cos, sin, u)


def _masked_q(q_ref, head, width):
    per_group = LANES // width
    g = head // per_group
    qg = q_ref[0, 0, LANES * g:LANES * (g + 1), :]
    rows = lax.broadcasted_iota(jnp.int32, qg.shape, 0)
    r0 = width * (head % per_group)
    return jnp.where((rows >= r0) & (rows < r0 + width), qg, jnp.zeros_like(qg))


def _load_k(k_ref, kb, group, t):
    k0 = pl.multiple_of(kb * t, t)
    return k_ref[0, pl.ds(k0, t), LANES * group:LANES * (group + 1)]


def _load_v(v_ref, kb, head):
    return v_ref[0, kb, HEAD_DIM * head:HEAD_DIM * (head + 1), :]


def _softmax_step(s, carry, vblk):
    m, acc = carry
    t = s.shape[1]
    m_new = jnp.maximum(m, jnp.max(s, axis=0, keepdims=True))
    p = jnp.exp2(s - m_new).astype(BF16)
    alpha = jnp.exp2(m - m_new)
    vaug = jnp.concatenate([vblk, jnp.ones((ONES_ROWS, t), BF16)], axis=0)
    return m_new, alpha * acc + _dot(vaug, p)


def _causal_tile(t, strict):
    key = lax.broadcasted_iota(jnp.int32, (t, t), 0)
    qry = lax.broadcasted_iota(jnp.int32, (t, t), 1)
    return key < qry if strict else key <= qry


def _softmax_init(t):
    return jnp.full((1, t), NEG_BIG, F32), jnp.zeros((HEAD_DIM + ONES_ROWS, t), F32)


def _normalised(carry):
    acc = carry[1]
    return acc[0:HEAD_DIM] / acc[HEAD_DIM:HEAD_DIM + 1]


def _staggered(n, score_fn, step_fn, first=None, before_last=None):
    out = []
    s_next = score_fn(0) if first is None else first
    for h in range(n):
        s = s_next
        if h + 1 < n:
            s_next = score_fn(h + 1)
        elif before_last is not None:
            before_last()
        out.append(step_fn(h, s))
    return tuple(out)


def _softmax_blocks(n, score, step, s0_ref, lo, qb, t):
    def body(kb, carries, diag):
        def prefetch():
            s0_ref[...] = score(kb + 1, 0)

        return _staggered(n, lambda h: score(kb, h), lambda h, s: step(kb, h, s, carries[h], diag),
                          first=s0_ref[...], before_last=None if diag else prefetch)

    s0_ref[...] = score(lo, 0)
    carries = lax.fori_loop(lo, qb, lambda kb, c: body(kb, c, False), tuple(_softmax_init(t) for _ in range(n)))
    return body(qb, carries, True)


def _diff_kernel(q_ref, k_ref, v_ref, lam_ref, g_ref, o_ref, s0_ref, *, lambda_init):
    qb = pl.program_id(1)
    t = q_ref.shape[-1]
    n = 2 * N_HEADS
    per_group = LANES // DIFF_DIM
    lam_p = lam_ref[...]
    lam = (jnp.exp(jnp.sum(lam_p[0:1] * lam_p[1:2], keepdims=True))
           - jnp.exp(jnp.sum(lam_p[2:3] * lam_p[3:4], keepdims=True)) + lambda_init)
    qms = [_masked_q(q_ref, head, DIFF_DIM) for head in range(n)]

    def score(kb, head):
        return _dot(_load_k(k_ref, kb, head // per_group, t), qms[head])

    def step(kb, head, s, carry, diag):
        if diag:
            s = jnp.where(_causal_tile(t, False), s, NEG_BIG)
        return _softmax_step(s, carry, _load_v(v_ref, kb, head // 2))

    comps = [_normalised(c) for c in _softmax_blocks(n, score, step, s0_ref, 0, qb, t)]
    outs = []
    for hh in range(N_HEADS):
        o = comps[2 * hh] - lam * comps[2 * hh + 1]
        o = o * lax.rsqrt(jnp.mean(o * o, axis=0, keepdims=True) + LN_EPS)
        outs.append(o * (g_ref[...] * (1.0 - lambda_init)))
    o_ref[0] = jnp.concatenate(outs, axis=0).T.astype(BF16)


def _stick_step(z, carry, vblk, ut, diag):
    run, acc = carry
    t = z.shape[0]
    sub = ut.shape[0]
    lk = -(jnp.maximum(z, 0.0) + jnp.log(1.0 + jnp.exp(-jnp.abs(z))))
    lkm = jnp.where(_causal_tile(t, True), lk, 0.0) if diag else lk
    hi = lkm.astype(BF16)
    lo = (lkm - hi.astype(F32)).astype(BF16)
    later = jnp.zeros((1, t), F32)
    parts = []
    for i in reversed(range(t // sub)):
        rows = slice(i * sub, (i + 1) * sub)
        within = _dot(ut, hi[rows]) + _dot(ut, lo[rows])
        parts.append(within + later)
        later = later + within[0:1] + lkm[i * sub:i * sub + 1]
    after = jnp.concatenate(parts[::-1], axis=0)
    a = jnp.exp(z + lk + after + run)
    if diag:
        a = jnp.where(_causal_tile(t, True), a, 0.0)
    return run + later, acc + _dot(vblk, a.astype(BF16))


def _stick_kernel(q_ref, k_ref, v_ref, ut_ref, o_ref):
    qb = pl.program_id(1)
    t = q_ref.shape[-1]
    per_group = LANES // HEAD_DIM
    qms = [_masked_q(q_ref, head, HEAD_DIM) for head in range(N_HEADS)]

    def body(kb, carries, diag):
        kblks = [_load_k(k_ref, kb, g, t) for g in range(N_HEADS // per_group)]
        return _staggered(N_HEADS, lambda head: _dot(kblks[head // per_group], qms[head]),
                          lambda head, z: _stick_step(z, carries[head], _load_v(v_ref, kb, head), ut_ref[...], diag))

    def live(state):
        i, carries = state
        worst = functools.reduce(jnp.maximum, [c[0] for c in carries])
        return jnp.logical_and(i < qb, jnp.max(worst) > STICK_EXIT)

    init = tuple((jnp.zeros((1, t), F32), jnp.zeros((HEAD_DIM, t), F32)) for _ in range(N_HEADS))
    _, carries = lax.while_loop(live, lambda st: (st[0] + 1, body(qb - 1 - st[0], st[1], False)),
                                (jnp.int32(0), body(qb, init, True)))
    o_ref[0] = jnp.concatenate([c[1] for c in carries], axis=0).T.astype(BF16)


def _dilated_kernel(q_ref, k_ref, v_ref, bias_ref, o_ref, s0_ref):
    qb = pl.program_id(1)
    t = q_ref.shape[-1]
    n_back = bias_ref.shape[0] - 1
    per_group = LANES // HEAD_DIM
    qms = [_masked_q(q_ref, head, HEAD_DIM) for head in range(N_HEADS)]

    def score(kb, head):
        return _dot(_load_k(k_ref, kb, head // per_group, t), qms[head])

    def step(kb, head, s, carry, diag):
        return _softmax_step(bias_ref[qb - kb] + s, carry, _load_v(v_ref, kb, head))

    carries = _softmax_blocks(N_HEADS, score, step, s0_ref, jnp.maximum(qb - n_back, 0), qb, t)
    o_ref[0] = jnp.concatenate([_normalised(c) for c in carries], axis=0).T.astype(BF16)


def _forget_kernel(q_ref, k_ref, v_ref, o_ref, s0_ref):
    qb = pl.program_id(1)
    t = q_ref.shape[-1]
    qas = [q_ref[0, 0, AUG * head:AUG * (head + 1), :] for head in range(N_HEADS)]

    def score(kb, head):
        return _dot(_load_k(k_ref, kb, head, t), qas[head])

    def step(kb, head, s, carry, diag):
        if diag:
            s = jnp.where(_causal_tile(t, False), s, NEG_BIG)
        return _softmax_step(s, carry, _load_v(v_ref, kb, head))

    carries = _softmax_blocks(N_HEADS, score, step, s0_ref, 0, qb, t)
    o_ref[0] = jnp.concatenate([_normalised(c) for c in carries], axis=0).T.astype(BF16)


def _dilated_bias(t):
    n_back = max(w for w, _ in DILATED_PATTERNS) // t
    kk = np.arange(t)[:, None]
    qq = np.arange(t)[None, :]
    tiles = []
    for d in range(n_back + 1):
        dist = d * t + qq - kk
        count = np.zeros((t, t), np.float64)
        for window, dil in DILATED_PATTERNS:
            count += (dist >= 0) & (dist % dil == 0) & (dist // dil <= window // dil)
        tiles.append(np.where(count > 0, np.log2(np.maximum(count, 1.0)), NEG_BIG))
    return jnp.asarray(np.stack(tiles), F32)


def _attention(kind, branch, qv, k, extra_inputs, extra_specs, kernel, l, scratch=()):
    b, ns, _, t = qv.shape
    s = ns * t
    n_aug = N_HEADS * AUG
    if branch == 3:
        width, qk_block = n_aug, 0
    else:
        width, qk_block = BRANCH_WIDTH, n_aug // BRANCH_WIDTH + branch
    v_block = (n_aug + 3 * BRANCH_WIDTH) // BRANCH_WIDTH + branch
    in_specs = [
        pl.BlockSpec((1, 1, width, t), lambda i, j: (i, j, qk_block, 0)),
        pl.BlockSpec((1, s, width), lambda i, j: (i, 0, qk_block)),
        pl.BlockSpec((1, ns, BRANCH_WIDTH, t), lambda i, j: (i, 0, v_block, 0)),
    ] + extra_specs
    return pl.pallas_call(
        kernel,
        grid=(b, ns),
        in_specs=in_specs,
        out_specs=pl.BlockSpec((1, t, BRANCH_WIDTH), lambda i, j: (i, j, 0)),
        out_shape=jax.ShapeDtypeStruct((b, s, BRANCH_WIDTH), BF16),
        scratch_shapes=list(scratch),
        compiler_params=_params("arbitrary", "arbitrary"),
        name=f"{kind}_attention_l{l}",
    )(qv, k, qv, *extra_inputs)


def _merge_kernel(x_ref, sc_ref, sh_ref, ga_ref, oa_ref, ob_ref, oc_ref, od_ref, wg_ref, wb_ref, wo_ref,
                  lng_ref, lnb_ref, out_ref):
    x = x_ref[0]
    d = x.shape[-1]
    h = (x * (1.0 + sc_ref[0, 0]) + sh_ref[0, 0]).astype(BF16)
    merged = None
    for n, o_ref in enumerate((oa_ref, ob_ref, oc_ref, od_ref)):
        gate = 1.0 / (1.0 + jnp.exp(-_dot(h, wg_ref[:, d * n:d * (n + 1)])))
        y = gate * _dot(o_ref[0], wb_ref[n])
        merged = y if merged is None else merged + y
    mix = _dot(merged.astype(BF16), wo_ref[...])
    out_ref[0] = _layer_norm(DEEPNORM_ALPHA * x + ga_ref[0, 0] * mix, lng_ref[...], lnb_ref[...])


def _merge(x, mod, outs, wg, wb, wo, ln_g, ln_b, l):
    b, s, d = x.shape
    t = SEQ_TILE
    const = lambda shape: pl.BlockSpec(shape, lambda i, j: (0,) * len(shape))
    mod_spec = lambda idx: pl.BlockSpec((1, 1, 1, d), lambda i, j: (i, idx, 0, 0))
    o_spec = pl.BlockSpec((1, t, BRANCH_WIDTH), lambda i, j: (i, j, 0))
    return pl.pallas_call(
        _merge_kernel,
        grid=(b, s // t),
        in_specs=[pl.BlockSpec((1, t, d), lambda i, j: (i, j, 0)), mod_spec(1), mod_spec(0), mod_spec(2),
                  o_spec, o_spec, o_spec, o_spec,
                  const((d, N_BRANCHES * d)), const((N_BRANCHES, BRANCH_WIDTH, d)), const((d, d)),
                  const((1, d)), const((1, d))],
        out_specs=pl.BlockSpec((1, t, d), lambda i, j: (i, j, 0)),
        out_shape=jax.ShapeDtypeStruct((b, s, d), F32),
        compiler_params=_params("arbitrary", "arbitrary"),
        name=f"merge_l{l}",
    )(x, mod, mod, mod, *outs, wg, wb, wo, ln_g.reshape(1, d), ln_b.reshape(1, d))


def _ffn_kernel(x_ref, sc_ref, sh_ref, gf_ref, wup_ref, cw_ref, cb_ref, wdn_ref, lng_ref, lnb_ref,
                out_ref, tail_ref):
    t = x_ref.shape[1]
    chunk = D_FF // 2

    @pl.when(pl.program_id(1) == 0)
    def _():
        tail_ref[...] = jnp.zeros_like(tail_ref)

    x = x_ref[0]
    h = (x * (1.0 + sc_ref[0, 0]) + sh_ref[0, 0]).astype(BF16)
    rows = lax.broadcasted_iota(jnp.int32, (8, chunk), 0)

    def conv(col0):
        cols = slice(col0, col0 + chunk)
        u = _dot(h, wup_ref[:, cols])
        tail = tail_ref[:, cols]
        u1, u2 = pltpu.roll(u, 1, 0), pltpu.roll(u, 2, 0)
        head1 = jnp.where(rows < 1, pltpu.roll(tail, 1, 0), u1[0:8])
        head2 = jnp.where(rows < 2, pltpu.roll(tail, 2, 0), u2[0:8])
        u1 = jnp.concatenate([head1, u1[8:]], axis=0)
        u2 = jnp.concatenate([head2, u2[8:]], axis=0)
        tail_ref[:, cols] = u[t - 8:t]
        return cw_ref[2:3, cols] * u + cw_ref[1:2, cols] * u1 + cw_ref[0:1, cols] * u2 + cb_ref[:, cols]

    ffn = None
    for c in range(D_FF // chunk):
        a = conv(c * chunk)
        g = conv(D_FF + c * chunk)
        act = (a / (1.0 + jnp.exp(-a)) * g).astype(BF16)
        y = _dot(act, wdn_ref[c * chunk:(c + 1) * chunk, :])
        ffn = y if ffn is None else ffn + y
    out_ref[0] = _layer_norm(DEEPNORM_ALPHA * x + gf_ref[0, 0] * ffn, lng_ref[...], lnb_ref[...])


def _conv_ffn(x, mod, wup, cw, cb, wdn, ln_g, ln_b, l):
    b, s, d = x.shape
    t = SEQ_TILE
    const = lambda shape: pl.BlockSpec(shape, lambda i, j: (0,) * len(shape))
    mod_spec = lambda idx: pl.BlockSpec((1, 1, 1, d), lambda i, j: (i, idx, 0, 0))
    return pl.pallas_call(
        _ffn_kernel,
        grid=(b, s // t),
        in_specs=[pl.BlockSpec((1, t, d), lambda i, j: (i, j, 0)), mod_spec(4), mod_spec(3), mod_spec(5),
                  const((d, 2 * D_FF)), const((CONV_WIDTH, 2 * D_FF)), const((1, 2 * D_FF)), const((D_FF, d)),
                  const((1, d)), const((1, d))],
        out_specs=pl.BlockSpec((1, t, d), lambda i, j: (i, j, 0)),
        out_shape=jax.ShapeDtypeStruct((b, s, d), F32),
        scratch_shapes=[pltpu.VMEM((8, 2 * D_FF), F32)],
        compiler_params=_params("arbitrary", "arbitrary"),
        name=f"conv_ffn_l{l}",
    )(x, mod, mod, mod, wup, cw, cb.reshape(1, 2 * D_FF), wdn, ln_g.reshape(1, d), ln_b.reshape(1, d))


def kernel(x, c, positions, w_ada, b_ada, w_in, lam_q1, lam_k1, lam_q2, lam_k2, subln_g, forget_b, w_branch,
           w_o, ln1_g, ln1_b, w_up, conv_w, conv_b, w_down, ln2_g, ln2_b):
    b, s, d = x.shape
    t = SEQ_TILE
    bw = BRANCH_WIDTH
    const = lambda shape: pl.BlockSpec(shape, lambda i, j: (0,) * len(shape))

    mod = _modulation(c, w_ada, b_ada)
    cos, sin = _rope_tables(positions)
    bias = _dilated_bias(t)
    sub = min(STICK_SUB, t)
    ut = jnp.asarray(np.triu(np.ones((sub, sub), np.float32), 1), BF16)
    ahead = [pltpu.VMEM((t, t), F32)]

    for l in range(DEPTH):
        lambda_init = 0.8 - 0.6 * math.exp(-0.3 * l)
        w = w_in[l]
        col = lambda i: w[:, bw * i:bw * (i + 1)]
        wq = jnp.concatenate([col(0) * (DIFF_DIM ** -0.5 * LOG2E), col(3) * HEAD_DIM ** -0.5,
                              col(6) * (HEAD_DIM ** -0.5 * LOG2E), col(9) * (HEAD_DIM ** -0.5 * LOG2E)],
                             axis=1).T.astype(BF16)
        wk = jnp.concatenate([col(1), col(4), col(7), col(10)], axis=1).T.astype(BF16)
        wf = jnp.zeros((d, F_ROWS), F32).at[:, :N_HEADS].set(w[:, 12 * bw:12 * bw + N_HEADS])
        wv = jnp.concatenate([col(2), col(5), col(8), col(11), wf], axis=1).T.astype(BF16)
        wg = w[:, 12 * bw + N_HEADS:].astype(BF16)
        fb = jnp.zeros((F_ROWS, 1), F32).at[:N_HEADS, 0].set(forget_b[l])

        qv, k = _projection(x, mod[l], mod[l], wq, wk, wv, fb, cos, sin, l)

        lam = jnp.stack([lam_q1[l], lam_k1[l], lam_q2[l], lam_k2[l]])
        oa = _attention("diff", 0, qv, k, [lam, subln_g[l].reshape(HEAD_DIM, 1)],
                        [const((4, DIFF_DIM)), const((HEAD_DIM, 1))],
                        functools.partial(_diff_kernel, lambda_init=lambda_init), l, ahead)
        ob = _attention("stick", 1, qv, k, [ut], [const((sub, sub))], _stick_kernel, l)
        oc = _attention("dilated", 2, qv, k, [bias], [const(bias.shape)], _dilated_kernel, l, ahead)
        od = _attention("forget", 3, qv, k, [], [], _forget_kernel, l, ahead)

        x = _merge(x, mod[l], (oa, ob, oc, od), wg, w_branch[l].astype(BF16), w_o[l].astype(BF16),
                   ln1_g[l], ln1_b[l], l)
        x = _conv_ffn(x, mod[l], w_up[l].astype(BF16), conv_w[l], conv_b[l], w_down[l].astype(BF16),
                      ln2_g[l], ln2_b[l], l)
    return x
```

```python
import functools
import math

import numpy as np
import jax
import jax.numpy as jnp
from jax import lax
from jax.experimental import pallas as pl
from jax.experimental.pallas import tpu as pltpu

D_MODEL = 1024
DEPTH = 2
HEAD_DIM = 64
DIFF_DIM = HEAD_DIM // 2
N_HEADS = 4
BRANCH_WIDTH = N_HEADS * HEAD_DIM
N_BRANCHES = 4
ROPE_THETA = 500000.0
ROPE_FRACTION = 4
DILATED_PATTERNS = ((128, 1), (512, 4), (2048, 16))
D_FF = 2048
CONV_WIDTH = 3
LN_EPS = 1e-5
DEEPNORM_ALPHA = (2.0 * DEPTH) ** 0.25

SEQ_TILE = 512
LANES = 128
F_ROWS = 16
ONES_ROWS = 16
NEG_BIG = -1e30
LOG2E = math.log2(math.e)
SOFTMAX_MARGIN = 64.0
STICK_EXIT = -105.0
STICK_SUB = 256
AUG = LANES
VMEM_LIMIT = 56 * 2**20

F32 = jnp.float32
BF16 = jnp.bfloat16
_NT = (((1,), (1,)), ((), ()))


def _dot(a, b):
    return jnp.dot(a, b, preferred_element_type=F32)


def _params(*sem):
    return pltpu.CompilerParams(dimension_semantics=sem, vmem_limit_bytes=VMEM_LIMIT)


def _split3(x):
    hi = x.astype(BF16)
    r1 = x - hi.astype(F32)
    mid = r1.astype(BF16)
    lo = (r1 - mid.astype(F32)).astype(BF16)
    return hi, mid, lo


def _layer_norm(r, g, b):
    mu = jnp.mean(r, axis=-1, keepdims=True)
    d = r - mu
    var = jnp.mean(d * d, axis=-1, keepdims=True)
    return d * lax.rsqrt(var + LN_EPS) * g + b


def _mod_kernel(c_ref, w_ref, b_ref, o_ref):
    c = c_ref[...]
    a = c / (1.0 + jnp.exp(-c))
    a_hi = a.astype(BF16)
    a_lo = (a - a_hi.astype(F32)).astype(BF16)
    w = w_ref[0]
    w_hi = w.astype(BF16)
    w_lo = (w - w_hi.astype(F32)).astype(BF16)
    o_ref[0] = _dot(a_hi, w_hi) + _dot(a_lo, w_hi) + _dot(a_hi, w_lo) + b_ref[0]


def _modulation(c, w_ada, b_ada):
    b = c.shape[0]
    rows = 8
    c_pad = jnp.zeros((rows, D_MODEL), F32).at[:b].set(c)
    out = pl.pallas_call(
        _mod_kernel,
        grid=(DEPTH, 6),
        in_specs=[
            pl.BlockSpec((rows, D_MODEL), lambda l, j: (0, 0)),
            pl.BlockSpec((1, D_MODEL, D_MODEL), lambda l, j: (l, 0, j)),
            pl.BlockSpec((1, 1, D_MODEL), lambda l, j: (l, 0, j)),
        ],
        out_specs=pl.BlockSpec((1, rows, D_MODEL), lambda l, j: (l, 0, j)),
        out_shape=jax.ShapeDtypeStruct((DEPTH, rows, 6 * D_MODEL), F32),
        compiler_params=_params("arbitrary", "arbitrary"),
        name="adaln_mod",
    )(c_pad, w_ada, b_ada.reshape(DEPTH, 1, 6 * D_MODEL))
    return out[:, :b].reshape(DEPTH, b, 6, 1, D_MODEL)


def _rope_kernel(pos_ref, invf_ref, sgn_ref, cos_ref, sin_ref):
    ang = invf_ref[...] * pos_ref[0].astype(F32)
    cos_ref[0] = jnp.cos(ang)
    sin_ref[0] = jnp.sin(ang) * sgn_ref[...]


def _rope_tables(positions):
    b, s = positions.shape
    r_a, r_c = DIFF_DIM // ROPE_FRACTION, HEAD_DIM // ROPE_FRACTION
    inv_a = jnp.power(jnp.float32(ROPE_THETA), -2.0 * jnp.arange(r_a // 2, dtype=F32) / r_a)
    inv_c = jnp.power(jnp.float32(ROPE_THETA), -2.0 * jnp.arange(r_c // 2, dtype=F32) / r_c)
    invf = jnp.concatenate([inv_a, inv_a, inv_c]).reshape(16, 1)
    sgn = jnp.concatenate([-jnp.ones(4, F32), jnp.ones(12, F32)]).reshape(16, 1)
    return pl.pallas_call(
        _rope_kernel,
        grid=(b,),
        in_specs=[
            pl.BlockSpec((1, 1, s), lambda i: (i, 0, 0)),
            pl.BlockSpec((16, 1), lambda i: (0, 0)),
            pl.BlockSpec((16, 1), lambda i: (0, 0)),
        ],
        out_specs=[pl.BlockSpec((1, 16, s), lambda i: (i, 0, 0))] * 2,
        out_shape=[jax.ShapeDtypeStruct((b, 16, s), F32)] * 2,
        compiler_params=_params("arbitrary"),
        name="rope_tables",
    )(positions.reshape(b, 1, s), invf, sgn)


def _rope_rows(y, cos, sin):
    cos_a, sin_a, cos_c, sin_c = cos[0:8], sin[0:8], cos[8:16], sin[8:16]
    parts = []
    for hh in range(2 * N_HEADS):
        r0 = DIFF_DIM * hh
        seg = y[r0:r0 + 8]
        parts.append(seg * cos_a + pltpu.roll(seg, 4, 0) * sin_a)
        parts.append(y[r0 + 8:r0 + DIFF_DIM])
    parts.append(y[BRANCH_WIDTH:2 * BRANCH_WIDTH])
    for hh in range(N_HEADS):
        r0 = 2 * BRANCH_WIDTH + HEAD_DIM * hh
        t0, t1 = y[r0:r0 + 8], y[r0 + 8:r0 + 16]
        parts.append(t0 * cos_c - t1 * sin_c)
        parts.append(t1 * cos_c + t0 * sin_c)
        parts.append(y[r0 + 16:r0 + HEAD_DIM])
    parts.append(y[3 * BRANCH_WIDTH:4 * BRANCH_WIDTH])
    return jnp.concatenate(parts, axis=0)


def _proj_kernel(x_ref, sc_ref, sh_ref, wq_ref, wk_ref, wv_ref, fb_ref, cos_ref, sin_ref, u_ref,
                 qv_ref, k_ref, carry_ref):
    t = x_ref.shape[1]
    nq = 4 * BRANCH_WIDTH
    n3 = 3 * BRANCH_WIDTH

    @pl.when(pl.program_id(1) == 0)
    def _():
        carry_ref[...] = jnp.zeros_like(carry_ref)

    h = (x_ref[0] * (1.0 + sc_ref[0, 0]) + sh_ref[0, 0]).astype(BF16)
    cos, sin = cos_ref[0], sin_ref[0]

    yq = _rope_rows(lax.dot_general(wq_ref[...], h, _NT, preferred_element_type=F32), cos, sin)
    yk = _rope_rows(lax.dot_general(wk_ref[...], h, _NT, preferred_element_type=F32), cos, sin)
    yv = lax.dot_general(wv_ref[...], h, _NT, preferred_element_type=F32)

    fl = yv[nq:nq + F_ROWS] + fb_ref[...]
    lf = jnp.minimum(fl, 0.0) - jnp.log(1.0 + jnp.exp(-jnp.abs(fl)))
    cs = _dot(jnp.concatenate(_split3(lf), axis=0), u_ref[...])
    fcum = cs[0:F_ROWS] + cs[F_ROWS:2 * F_ROWS] + cs[2 * F_ROWS:3 * F_ROWS] + carry_ref[:, 0:1]
    carry_ref[...] = jnp.broadcast_to(fcum[:, t - 1:t], carry_ref.shape)

    f2 = fcum * LOG2E
    row8 = lax.broadcasted_iota(jnp.int32, (8, t), 0)
    ones3 = jnp.where(row8 < 3, 1.0, 0.0)
    pad = jnp.zeros((AUG - HEAD_DIM - 16, t), F32)
    q_rows, k_rows = [], []
    for hh in range(N_HEADS):
        hi, mid, lo = (p.astype(F32) for p in _split3(f2[hh:hh + 1]))
        pieces = jnp.where(row8 == 0, hi, jnp.where(row8 == 1, mid, jnp.where(row8 == 2, lo, 0.0)))
        feat = slice(n3 + HEAD_DIM * hh, n3 + HEAD_DIM * (hh + 1))
        q_rows += [yq[feat], ones3, pieces, pad]
        k_rows += [yk[feat], -pieces, ones3, pad]
    qv_ref[0, 0, 0:N_HEADS * AUG + n3, :] = jnp.concatenate(q_rows + [yq[0:n3]], axis=0).astype(BF16)
    qv_ref[0, 0, N_HEADS * AUG + n3:N_HEADS * AUG + n3 + nq, :] = yv[0:nq].astype(BF16)
    k_ref[0] = jnp.concatenate(k_rows + [yk[0:n3]], axis=0).T.astype(BF16)


def _projection(x, sc, sh, wq, wk, wv, fb, cos, sin, l):
    b, s, d = x.shape
    t = SEQ_TILE
    ns = s // t
    nq = 4 * BRANCH_WIDTH
    nk = N_HEADS * AUG + 3 * BRANCH_WIDTH
    u = jnp.asarray(np.triu(np.ones((t, t), np.float32)), BF16)
    const = lambda shape: pl.BlockSpec(shape, lambda i, j: (0,) * len(shape))
    return pl.pallas_call(
        _proj_kernel,
        grid=(b, ns),
        in_specs=[
            pl.BlockSpec((1, t, d), lambda i, j: (i, j, 0)),
            pl.BlockSpec((1, 1, 1, d), lambda i, j: (i, 1, 0, 0)),
            pl.BlockSpec((1, 1, 1, d), lambda i, j: (i, 0, 0, 0)),
            const((nq, d)), const((nq, d)), const((nq + F_ROWS, d)),
            const((F_ROWS, 1)),
            pl.BlockSpec((1, 16, t), lambda i, j: (i, 0, j)),
            pl.BlockSpec((1, 16, t), lambda i, j: (i, 0, j)),
            const((t, t)),
        ],
        out_specs=[
            pl.BlockSpec((1, 1, nk + nq, t), lambda i, j: (i, j, 0, 0)),
            pl.BlockSpec((1, t, nk), lambda i, j: (i, j, 0)),
        ],
        out_shape=[
            jax.ShapeDtypeStruct((b, ns, nk + nq, t), BF16),
            jax.ShapeDtypeStruct((b, s, nk), BF16),
        ],
        scratch_shapes=[pltpu.VMEM((F_ROWS, LANES), F32)],
        compiler_params=_params("arbitrary", "arbitrary"),
        name=f"projection_l{l}",
    )(x, sc, sh, wq, wk, wv, fb, cos, sin, u)


def _masked_q(q_ref, head, width):
    per_group = LANES // width
    g = head // per_group
    qg = q_ref[0, 0, LANES * g:LANES * (g + 1), :]
    rows = lax.broadcasted_iota(jnp.int32, qg.shape, 0)
    r0 = width * (head % per_group)
    return jnp.where((rows >= r0) & (rows < r0 + width), qg, jnp.zeros_like(qg))


def _load_k(k_ref, kb, group, t):
    k0 = pl.multiple_of(kb * t, t)
    return k_ref[0, pl.ds(k0, t), LANES * group:LANES * (group + 1)]


def _load_v(v_ref, kb, head):
    return v_ref[0, kb, HEAD_DIM * head:HEAD_DIM * (head + 1), :]


def _softmax_step(s, carry, vblk):
    m, acc = carry
    t = s.shape[1]
    m_new = jnp.maximum(m, jnp.max(s, axis=0, keepdims=True))
    p = jnp.exp2(s - m_new).astype(BF16)
    alpha = jnp.exp2(m - m_new)
    vaug = jnp.concatenate([vblk, jnp.ones((ONES_ROWS, t), BF16)], axis=0)
    return m_new, alpha * acc + _dot(vaug, p)


def _causal_tile(t, strict):
    key = lax.broadcasted_iota(jnp.int32, (t, t), 0)
    qry = lax.broadcasted_iota(jnp.int32, (t, t), 1)
    return key < qry if strict else key <= qry


def _staggered(n, score_fn, step_fn, first=None, before_last=None):
    out = []
    s_next = score_fn(0) if first is None else first
    for h in range(n):
        s = s_next
        if h + 1 < n:
            s_next = score_fn(h + 1)
        elif before_last is not None:
            before_last()
        out.append(step_fn(h, s))
    return tuple(out)


def _softmax_blocks(n, score, adjust, vblock, scratch, lo, qb, t):
    s0_ref, m_ref, acc_ref = scratch
    ones = jnp.ones((ONES_ROWS, t), BF16)

    def prefetch_before(kb):
        def prefetch():
            s0_ref[...] = score(jnp.maximum(kb - 1, 0), 0)
        return prefetch

    def first_step(h, s):
        s = adjust(qb, h, s, True)
        m = jnp.max(s, axis=0, keepdims=True)
        m_ref[h] = m
        acc_ref[h] = _dot(jnp.concatenate([vblock(qb, h), ones], axis=0), jnp.exp2(s - m).astype(BF16))

    _staggered(n, lambda h: score(qb, h), first_step, before_last=prefetch_before(qb))

    def loop_body(i, _):
        kb = qb - 1 - i
        excess = []

        def fast_step(h, s):
            m = m_ref[h]
            s = adjust(kb, h, s, False)
            excess.append(jnp.max(s, axis=0, keepdims=True) - m)
            return _dot(jnp.concatenate([vblock(kb, h), ones], axis=0), jnp.exp2(s - m).astype(BF16))

        deltas = _staggered(n, lambda h: score(kb, h), fast_step, first=s0_ref[...],
                            before_last=prefetch_before(kb))
        over = jnp.max(functools.reduce(jnp.maximum, excess)) > SOFTMAX_MARGIN

        @pl.when(jnp.logical_not(over))
        def _():
            for h in range(n):
                acc_ref[h] += deltas[h]

        @pl.when(over)
        def _():
            def exact_step(h, s):
                m_ref[h], acc_ref[h] = _softmax_step(adjust(kb, h, s, False), (m_ref[h], acc_ref[h]), vblock(kb, h))

            _staggered(n, lambda h: score(kb, h), exact_step)

        return 0

    lax.fori_loop(0, qb - lo, loop_body, 0)
    return [acc_ref[h, 0:HEAD_DIM] / acc_ref[h, HEAD_DIM:HEAD_DIM + 1] for h in range(n)]


def _diff_kernel(q_ref, k_ref, v_ref, lam_ref, g_ref, o_ref, *scratch, lambda_init):
    qb = pl.program_id(1)
    t = q_ref.shape[-1]
    n = 2 * N_HEADS
    per_group = LANES // DIFF_DIM
    lam_p = lam_ref[...]
    lam = (jnp.exp(jnp.sum(lam_p[0:1] * lam_p[1:2], keepdims=True))
           - jnp.exp(jnp.sum(lam_p[2:3] * lam_p[3:4], keepdims=True)) + lambda_init)
    qms = [_masked_q(q_ref, head, DIFF_DIM) for head in range(n)]

    def score(kb, head):
        return _dot(_load_k(k_ref, kb, head // per_group, t), qms[head])

    def adjust(kb, head, s, diag):
        return jnp.where(_causal_tile(t, False), s, NEG_BIG) if diag else s

    comps = _softmax_blocks(n, score, adjust, lambda kb, head: _load_v(v_ref, kb, head // 2), scratch, 0, qb, t)
    outs = []
    for hh in range(N_HEADS):
        o = comps[2 * hh] - lam * comps[2 * hh + 1]
        o = o * lax.rsqrt(jnp.mean(o * o, axis=0, keepdims=True) + LN_EPS)
        outs.append(o * (g_ref[...] * (1.0 - lambda_init)))
    o_ref[0] = jnp.concatenate(outs, axis=0).T.astype(BF16)


def _stick_step(z, carry, vblk, ut, diag):
    run, acc = carry
    t = z.shape[0]
    sub = ut.shape[0]
    lk = -(jnp.maximum(z, 0.0) + jnp.log(1.0 + jnp.exp(-jnp.abs(z))))
    lkm = jnp.where(_causal_tile(t, True), lk, 0.0) if diag else lk
    hi = lkm.astype(BF16)
    lo = (lkm - hi.astype(F32)).astype(BF16)
    later = jnp.zeros((1, t), F32)
    parts = []
    for i in reversed(range(t // sub)):
        rows = slice(i * sub, (i + 1) * sub)
        within = _dot(ut, hi[rows]) + _dot(ut, lo[rows])
        parts.append(within + later)
        later = later + within[0:1] + lkm[i * sub:i * sub + 1]
    after = jnp.concatenate(parts[::-1], axis=0)
    a = jnp.exp(z + lk + after + run)
    if diag:
        a = jnp.where(_causal_tile(t, True), a, 0.0)
    return run + later, acc + _dot(vblk, a.astype(BF16))


def _stick_kernel(q_ref, k_ref, v_ref, ut_ref, o_ref):
    qb = pl.program_id(1)
    t = q_ref.shape[-1]
    per_group = LANES // HEAD_DIM
    qms = [_masked_q(q_ref, head, HEAD_DIM) for head in range(N_HEADS)]

    def body(kb, carries, diag):
        kblks = [_load_k(k_ref, kb, g, t) for g in range(N_HEADS // per_group)]
        return _staggered(N_HEADS, lambda head: _dot(kblks[head // per_group], qms[head]),
                          lambda head, z: _stick_step(z, carries[head], _load_v(v_ref, kb, head), ut_ref[...], diag))

    def live(state):
        i, carries = state
        worst = functools.reduce(jnp.maximum, [c[0] for c in carries])
        return jnp.logical_and(i < qb, jnp.max(worst) > STICK_EXIT)

    init = tuple((jnp.zeros((1, t), F32), jnp.zeros((HEAD_DIM, t), F32)) for _ in range(N_HEADS))
    _, carries = lax.while_loop(live, lambda st: (st[0] + 1, body(qb - 1 - st[0], st[1], False)),
                                (jnp.int32(0), body(qb, init, True)))
    o_ref[0] = jnp.concatenate([c[1] for c in carries], axis=0).T.astype(BF16)


def _dilated_kernel(q_ref, k_ref, v_ref, bias_ref, o_ref, *scratch):
    qb = pl.program_id(1)
    t = q_ref.shape[-1]
    n_back = bias_ref.shape[0] - 1
    per_group = LANES // HEAD_DIM
    qms = [_masked_q(q_ref, head, HEAD_DIM) for head in range(N_HEADS)]

    def score(kb, head):
        return _dot(_load_k(k_ref, kb, head // per_group, t), qms[head])

    def adjust(kb, head, s, diag):
        return bias_ref[qb - kb] + s

    outs = _softmax_blocks(N_HEADS, score, adjust, lambda kb, head: _load_v(v_ref, kb, head), scratch,
                           jnp.maximum(qb - n_back, 0), qb, t)
    o_ref[0] = jnp.concatenate(outs, axis=0).T.astype(BF16)


def _forget_kernel(q_ref, k_ref, v_ref, o_ref, *scratch):
    qb = pl.program_id(1)
    t = q_ref.shape[-1]
    qas = [q_ref[0, 0, AUG * head:AUG * (head + 1), :] for head in range(N_HEADS)]

    def score(kb, head):
        return _dot(_load_k(k_ref, kb, head, t), qas[head])

    def adjust(kb, head, s, diag):
        return jnp.where(_causal_tile(t, False), s, NEG_BIG) if diag else s

    outs = _softmax_blocks(N_HEADS, score, adjust, lambda kb, head: _load_v(v_ref, kb, head), scratch, 0, qb, t)
    o_ref[0] = jnp.concatenate(outs, axis=0).T.astype(BF16)


def _dilated_bias(t):
    n_back = max(w for w, _ in DILATED_PATTERNS) // t
    kk = np.arange(t)[:, None]
    qq = np.arange(t)[None, :]
    tiles = []
    for d in range(n_back + 1):
        dist = d * t + qq - kk
        count = np.zeros((t, t), np.float64)
        for window, dil in DILATED_PATTERNS:
            count += (dist >= 0) & (dist % dil == 0) & (dist // dil <= window // dil)
        tiles.append(np.where(count > 0, np.log2(np.maximum(count, 1.0)), NEG_BIG))
    return jnp.asarray(np.stack(tiles), F32)


def _attention(kind, branch, qv, k, extra_inputs, extra_specs, kernel, l, scratch=()):
    b, ns, _, t = qv.shape
    s = ns * t
    n_aug = N_HEADS * AUG
    if branch == 3:
        width, qk_block = n_aug, 0
    else:
        width, qk_block = BRANCH_WIDTH, n_aug // BRANCH_WIDTH + branch
    v_block = (n_aug + 3 * BRANCH_WIDTH) // BRANCH_WIDTH + branch
    in_specs = [
        pl.BlockSpec((1, 1, width, t), lambda i, j: (i, j, qk_block, 0)),
        pl.BlockSpec((1, s, width), lambda i, j: (i, 0, qk_block)),
        pl.BlockSpec((1, ns, BRANCH_WIDTH, t), lambda i, j: (i, 0, v_block, 0)),
    ] + extra_specs
    return pl.pallas_call(
        kernel,
        grid=(b, ns),
        in_specs=in_specs,
        out_specs=pl.BlockSpec((1, t, BRANCH_WIDTH), lambda i, j: (i, j, 0)),
        out_shape=jax.ShapeDtypeStruct((b, s, BRANCH_WIDTH), BF16),
        scratch_shapes=list(scratch),
        compiler_params=_params("arbitrary", "arbitrary"),
        name=f"{kind}_attention_l{l}",
    )(qv, k, qv, *extra_inputs)


def _merge_kernel(x_ref, sc_ref, sh_ref, ga_ref, oa_ref, ob_ref, oc_ref, od_ref, wg_ref, wb_ref, wo_ref,
                  lng_ref, lnb_ref, out_ref):
    x = x_ref[0]
    d = x.shape[-1]
    h = (x * (1.0 + sc_ref[0, 0]) + sh_ref[0, 0]).astype(BF16)
    merged = None
    for n, o_ref in enumerate((oa_ref, ob_ref, oc_ref, od_ref)):
        gate = 1.0 / (1.0 + jnp.exp(-_dot(h, wg_ref[:, d * n:d * (n + 1)])))
        y = gate * _dot(o_ref[0], wb_ref[n])
        merged = y if merged is None else merged + y
    mix = _dot(merged.astype(BF16), wo_ref[...])
    out_ref[0] = _layer_norm(DEEPNORM_ALPHA * x + ga_ref[0, 0] * mix, lng_ref[...], lnb_ref[...])


def _merge(x, mod, outs, wg, wb, wo, ln_g, ln_b, l):
    b, s, d = x.shape
    t = SEQ_TILE
    const = lambda shape: pl.BlockSpec(shape, lambda i, j: (0,) * len(shape))
    mod_spec = lambda idx: pl.BlockSpec((1, 1, 1, d), lambda i, j: (i, idx, 0, 0))
    o_spec = pl.BlockSpec((1, t, BRANCH_WIDTH), lambda i, j: (i, j, 0))
    return pl.pallas_call(
        _merge_kernel,
        grid=(b, s // t),
        in_specs=[pl.BlockSpec((1, t, d), lambda i, j: (i, j, 0)), mod_spec(1), mod_spec(0), mod_spec(2),
                  o_spec, o_spec, o_spec, o_spec,
                  const((d, N_BRANCHES * d)), const((N_BRANCHES, BRANCH_WIDTH, d)), const((d, d)),
                  const((1, d)), const((1, d))],
        out_specs=pl.BlockSpec((1, t, d), lambda i, j: (i, j, 0)),
        out_shape=jax.ShapeDtypeStruct((b, s, d), F32),
        compiler_params=_params("arbitrary", "arbitrary"),
        name=f"merge_l{l}",
    )(x, mod, mod, mod, *outs, wg, wb, wo, ln_g.reshape(1, d), ln_b.reshape(1, d))


def _ffn_kernel(x_ref, sc_ref, sh_ref, gf_ref, wup_ref, cw_ref, cb_ref, wdn_ref, lng_ref, lnb_ref,
                out_ref, tail_ref):
    t = x_ref.shape[1]
    chunk = D_FF // 2

    @pl.when(pl.program_id(1) == 0)
    def _():
        tail_ref[...] = jnp.zeros_like(tail_ref)

    x = x_ref[0]
    h = (x * (1.0 + sc_ref[0, 0]) + sh_ref[0, 0]).astype(BF16)
    rows = lax.broadcasted_iota(jnp.int32, (8, chunk), 0)

    def conv(col0):
        cols = slice(col0, col0 + chunk)
        u = _dot(h, wup_ref[:, cols])
        tail = tail_ref[:, cols]
        u1, u2 = pltpu.roll(u, 1, 0), pltpu.roll(u, 2, 0)
        head1 = jnp.where(rows < 1, pltpu.roll(tail, 1, 0), u1[0:8])
        head2 = jnp.where(rows < 2, pltpu.roll(tail, 2, 0), u2[0:8])
        u1 = jnp.concatenate([head1, u1[8:]], axis=0)
        u2 = jnp.concatenate([head2, u2[8:]], axis=0)
        tail_ref[:, cols] = u[t - 8:t]
        return cw_ref[2:3, cols] * u + cw_ref[1:2, cols] * u1 + cw_ref[0:1, cols] * u2 + cb_ref[:, cols]

    ffn = None
    for c in range(D_FF // chunk):
        a = conv(c * chunk)
        g = conv(D_FF + c * chunk)
        act = (a / (1.0 + jnp.exp(-a)) * g).astype(BF16)
        y = _dot(act, wdn_ref[c * chunk:(c + 1) * chunk, :])
        ffn = y if ffn is None else ffn + y
    out_ref[0] = _layer_norm(DEEPNORM_ALPHA * x + gf_ref[0, 0] * ffn, lng_ref[...], lnb_ref[...])


def _conv_ffn(x, mod, wup, cw, cb, wdn, ln_g, ln_b, l):
    b, s, d = x.shape
    t = SEQ_TILE
    const = lambda shape: pl.BlockSpec(shape, lambda i, j: (0,) * len(shape))
    mod_spec = lambda idx: pl.BlockSpec((1, 1, 1, d), lambda i, j: (i, idx, 0, 0))
    return pl.pallas_call(
        _ffn_kernel,
        grid=(b, s // t),
        in_specs=[pl.BlockSpec((1, t, d), lambda i, j: (i, j, 0)), mod_spec(4), mod_spec(3), mod_spec(5),
                  const((d, 2 * D_FF)), const((CONV_WIDTH, 2 * D_FF)), const((1, 2 * D_FF)), const((D_FF, d)),
                  const((1, d)), const((1, d))],
        out_specs=pl.BlockSpec((1, t, d), lambda i, j: (i, j, 0)),
        out_shape=jax.ShapeDtypeStruct((b, s, d), F32),
        scratch_shapes=[pltpu.VMEM((8, 2 * D_FF), F32)],
        compiler_params=_params("arbitrary", "arbitrary"),
        name=f"conv_ffn_l{l}",
    )(x, mod, mod, mod, wup, cw, cb.reshape(1, 2 * D_FF), wdn, ln_g.reshape(1, d), ln_b.reshape(1, d))


def kernel(x, c, positions, w_ada, b_ada, w_in, lam_q1, lam_k1, lam_q2, lam_k2, subln_g, forget_b, w_branch,
           w_o, ln1_g, ln1_b, w_up, conv_w, conv_b, w_down, ln2_g, ln2_b):
    b, s, d = x.shape
    t = SEQ_TILE
    bw = BRANCH_WIDTH
    const = lambda shape: pl.BlockSpec(shape, lambda i, j: (0,) * len(shape))

    mod = _modulation(c, w_ada, b_ada)
    cos, sin = _rope_tables(positions)
    bias = _dilated_bias(t)
    sub = min(STICK_SUB, t)
    ut = jnp.asarray(np.triu(np.ones((sub, sub), np.float32), 1), BF16)
    chains = lambda n: [pltpu.VMEM((t, t), F32), pltpu.VMEM((n, 1, t), F32),
                        pltpu.VMEM((n, HEAD_DIM + ONES_ROWS, t), F32)]

    for l in range(DEPTH):
        lambda_init = 0.8 - 0.6 * math.exp(-0.3 * l)
        w = w_in[l]
        col = lambda i: w[:, bw * i:bw * (i + 1)]
        wq = jnp.concatenate([col(0) * (DIFF_DIM ** -0.5 * LOG2E), col(3) * HEAD_DIM ** -0.5,
                              col(6) * (HEAD_DIM ** -0.5 * LOG2E), col(9) * (HEAD_DIM ** -0.5 * LOG2E)],
                             axis=1).T.astype(BF16)
        wk = jnp.concatenate([col(1), col(4), col(7), col(10)], axis=1).T.astype(BF16)
        wf = jnp.zeros((d, F_ROWS), F32).at[:, :N_HEADS].set(w[:, 12 * bw:12 * bw + N_HEADS])
        wv = jnp.concatenate([col(2), col(5), col(8), col(11), wf], axis=1).T.astype(BF16)
        wg = w[:, 12 * bw + N_HEADS:].astype(BF16)
        fb = jnp.zeros((F_ROWS, 1), F32).at[:N_HEADS, 0].set(forget_b[l])

        qv, k = _projection(x, mod[l], mod[l], wq, wk, wv, fb, cos, sin, l)

        lam = jnp.stack([lam_q1[l], lam_k1[l], lam_q2[l], lam_k2[l]])
        oa = _attention("diff", 0, qv, k, [lam, subln_g[l].reshape(HEAD_DIM, 1)],
                        [const((4, DIFF_DIM)), const((HEAD_DIM, 1))],
                        functools.partial(_diff_kernel, lambda_init=lambda_init), l, chains(2 * N_HEADS))
        ob = _attention("stick", 1, qv, k, [ut], [const((sub, sub))], _stick_kernel, l)
        oc = _attention("dilated", 2, qv, k, [bias], [const(bias.shape)], _dilated_kernel, l, chains(N_HEADS))
        od = _attention("forget", 3, qv, k, [], [], _forget_kernel, l, chains(N_HEADS))

        x = _merge(x, mod[l], (oa, ob, oc, od), wg, w_branch[l].astype(BF16), w_o[l].astype(BF16),
                   ln1_g[l], ln1_b[l], l)
        x = _conv_ffn(x, mod[l], w_up[l].astype(BF16), conv_w[l], conv_b[l], w_down[l].astype(BF16),
                      ln2_g[l], ln2_b[l], l)
    return x
```

```python
import functools
import math

import numpy as np
import jax
import jax.numpy as jnp
from jax import lax
from jax.experimental import pallas as pl
from jax.experimental.pallas import tpu as pltpu

D_MODEL = 1024
DEPTH = 2
HEAD_DIM = 64
DIFF_DIM = HEAD_DIM // 2
N_HEADS = 4
BRANCH_WIDTH = N_HEADS * HEAD_DIM
N_BRANCHES = 4
ROPE_THETA = 500000.0
ROPE_FRACTION = 4
DILATED_PATTERNS = ((128, 1), (512, 4), (2048, 16))
D_FF = 2048
CONV_WIDTH = 3
LN_EPS = 1e-5
DEEPNORM_ALPHA = (2.0 * DEPTH) ** 0.25

SEQ_TILE = 512
LANES = 128
F_ROWS = 16
ONES_ROWS = 16
NEG_BIG = -1e30
LOG2E = math.log2(math.e)
SOFTMAX_MARGIN = 64.0
STICK_EXIT = -105.0
AUG = LANES
VMEM_LIMIT = 56 * 2**20

F32 = jnp.float32
BF16 = jnp.bfloat16
_NT = (((1,), (1,)), ((), ()))


def _dot(a, b):
    return jnp.dot(a, b, preferred_element_type=F32)


def _params(*sem):
    return pltpu.CompilerParams(dimension_semantics=sem, vmem_limit_bytes=VMEM_LIMIT)


def _split3(x):
    hi = x.astype(BF16)
    r1 = x - hi.astype(F32)
    mid = r1.astype(BF16)
    lo = (r1 - mid.astype(F32)).astype(BF16)
    return hi, mid, lo


def _layer_norm(r, g, b):
    mu = jnp.mean(r, axis=-1, keepdims=True)
    d = r - mu
    var = jnp.mean(d * d, axis=-1, keepdims=True)
    return d * lax.rsqrt(var + LN_EPS) * g + b


def _mod_kernel(c_ref, w_ref, b_ref, o_ref):
    c = c_ref[...]
    a = c / (1.0 + jnp.exp(-c))
    a_hi = a.astype(BF16)
    a_lo = (a - a_hi.astype(F32)).astype(BF16)
    w = w_ref[0]
    w_hi = w.astype(BF16)
    w_lo = (w - w_hi.astype(F32)).astype(BF16)
    o_ref[0] = _dot(a_hi, w_hi) + _dot(a_lo, w_hi) + _dot(a_hi, w_lo) + b_ref[0]


def _modulation(c, w_ada, b_ada):
    b = c.shape[0]
    rows = 8
    c_pad = jnp.zeros((rows, D_MODEL), F32).at[:b].set(c)
    out = pl.pallas_call(
        _mod_kernel,
        grid=(DEPTH, 6),
        in_specs=[
            pl.BlockSpec((rows, D_MODEL), lambda l, j: (0, 0)),
            pl.BlockSpec((1, D_MODEL, D_MODEL), lambda l, j: (l, 0, j)),
            pl.BlockSpec((1, 1, D_MODEL), lambda l, j: (l, 0, j)),
        ],
        out_specs=pl.BlockSpec((1, rows, D_MODEL), lambda l, j: (l, 0, j)),
        out_shape=jax.ShapeDtypeStruct((DEPTH, rows, 6 * D_MODEL), F32),
        compiler_params=_params("arbitrary", "arbitrary"),
        name="adaln_mod",
    )(c_pad, w_ada, b_ada.reshape(DEPTH, 1, 6 * D_MODEL))
    return out[:, :b].reshape(DEPTH, b, 6, 1, D_MODEL)


def _rope_kernel(pos_ref, invf_ref, sgn_ref, cos_ref, sin_ref):
    ang = invf_ref[...] * pos_ref[0].astype(F32)
    cos_ref[0] = jnp.cos(ang)
    sin_ref[0] = jnp.sin(ang) * sgn_ref[...]


def _rope_tables(positions):
    b, s = positions.shape
    r_a, r_c = DIFF_DIM // ROPE_FRACTION, HEAD_DIM // ROPE_FRACTION
    inv_a = jnp.power(jnp.float32(ROPE_THETA), -2.0 * jnp.arange(r_a // 2, dtype=F32) / r_a)
    inv_c = jnp.power(jnp.float32(ROPE_THETA), -2.0 * jnp.arange(r_c // 2, dtype=F32) / r_c)
    invf = jnp.concatenate([inv_a, inv_a, inv_c]).reshape(16, 1)
    sgn = jnp.concatenate([-jnp.ones(4, F32), jnp.ones(12, F32)]).reshape(16, 1)
    return pl.pallas_call(
        _rope_kernel,
        grid=(b,),
        in_specs=[
            pl.BlockSpec((1, 1, s), lambda i: (i, 0, 0)),
            pl.BlockSpec((16, 1), lambda i: (0, 0)),
            pl.BlockSpec((16, 1), lambda i: (0, 0)),
        ],
        out_specs=[pl.BlockSpec((1, 16, s), lambda i: (i, 0, 0))] * 2,
        out_shape=[jax.ShapeDtypeStruct((b, 16, s), F32)] * 2,
        compiler_params=_params("arbitrary"),
        name="rope_tables",
    )(positions.reshape(b, 1, s), invf, sgn)


def _rope_rows(y, cos, sin):
    cos_a, sin_a, cos_c, sin_c = cos[0:8], sin[0:8], cos[8:16], sin[8:16]
    parts = []
    for hh in range(2 * N_HEADS):
        r0 = DIFF_DIM * hh
        seg = y[r0:r0 + 8]
        parts.append(seg * cos_a + pltpu.roll(seg, 4, 0) * sin_a)
        parts.append(y[r0 + 8:r0 + DIFF_DIM])
    parts.append(y[BRANCH_WIDTH:2 * BRANCH_WIDTH])
    for hh in range(N_HEADS):
        r0 = 2 * BRANCH_WIDTH + HEAD_DIM * hh
        t0, t1 = y[r0:r0 + 8], y[r0 + 8:r0 + 16]
        parts.append(t0 * cos_c - t1 * sin_c)
        parts.append(t1 * cos_c + t0 * sin_c)
        parts.append(y[r0 + 16:r0 + HEAD_DIM])
    parts.append(y[3 * BRANCH_WIDTH:4 * BRANCH_WIDTH])
    return jnp.concatenate(parts, axis=0)


def _proj_kernel(x_ref, sc_ref, sh_ref, wq_ref, wk_ref, wv_ref, fb_ref, cos_ref, sin_ref, u_ref,
                 qv_ref, k_ref, vs_ref, carry_ref):
    t = x_ref.shape[1]
    nq = 4 * BRANCH_WIDTH
    n3 = 3 * BRANCH_WIDTH

    @pl.when(pl.program_id(1) == 0)
    def _():
        carry_ref[...] = jnp.zeros_like(carry_ref)

    h = (x_ref[0] * (1.0 + sc_ref[0, 0]) + sh_ref[0, 0]).astype(BF16)
    cos, sin = cos_ref[0], sin_ref[0]

    yq = _rope_rows(lax.dot_general(wq_ref[...], h, _NT, preferred_element_type=F32), cos, sin)
    yk = _rope_rows(lax.dot_general(wk_ref[...], h, _NT, preferred_element_type=F32), cos, sin)
    yv = lax.dot_general(wv_ref[...], h, _NT, preferred_element_type=F32)

    fl = yv[nq:nq + F_ROWS] + fb_ref[...]
    lf = jnp.minimum(fl, 0.0) - jnp.log(1.0 + jnp.exp(-jnp.abs(fl)))
    cs = _dot(jnp.concatenate(_split3(lf), axis=0), u_ref[...])
    fcum = cs[0:F_ROWS] + cs[F_ROWS:2 * F_ROWS] + cs[2 * F_ROWS:3 * F_ROWS] + carry_ref[:, 0:1]
    carry_ref[...] = jnp.broadcast_to(fcum[:, t - 1:t], carry_ref.shape)

    f2 = fcum * LOG2E
    row8 = lax.broadcasted_iota(jnp.int32, (8, t), 0)
    ones3 = jnp.where(row8 < 3, 1.0, 0.0)
    pad = jnp.zeros((AUG - HEAD_DIM - 16, t), F32)
    q_rows, k_rows = [], []
    for hh in range(N_HEADS):
        hi, mid, lo = (p.astype(F32) for p in _split3(f2[hh:hh + 1]))
        pieces = jnp.where(row8 == 0, hi, jnp.where(row8 == 1, mid, jnp.where(row8 == 2, lo, 0.0)))
        feat = slice(n3 + HEAD_DIM * hh, n3 + HEAD_DIM * (hh + 1))
        q_rows += [yq[feat], ones3, pieces, pad]
        k_rows += [yk[feat], -pieces, ones3, pad]
    qv_ref[0, 0, 0:N_HEADS * AUG + n3, :] = jnp.concatenate(q_rows + [yq[0:n3]], axis=0).astype(BF16)
    qv_ref[0, 0, N_HEADS * AUG + n3:N_HEADS * AUG + n3 + nq, :] = yv[0:nq].astype(BF16)
    for half in range(2):
        vs_ref[0, half] = yv[BRANCH_WIDTH:2 * BRANCH_WIDTH, half * (t // 2):(half + 1) * (t // 2)].astype(BF16)
    k_ref[0] = jnp.concatenate(k_rows + [yk[0:n3]], axis=0).T.astype(BF16)


def _projection(x, sc, sh, wq, wk, wv, fb, cos, sin, l):
    b, s, d = x.shape
    t = SEQ_TILE
    ns = s // t
    nq = 4 * BRANCH_WIDTH
    nk = N_HEADS * AUG + 3 * BRANCH_WIDTH
    u = jnp.asarray(np.triu(np.ones((t, t), np.float32)), BF16)
    const = lambda shape: pl.BlockSpec(shape, lambda i, j: (0,) * len(shape))
    return pl.pallas_call(
        _proj_kernel,
        grid=(b, ns),
        in_specs=[
            pl.BlockSpec((1, t, d), lambda i, j: (i, j, 0)),
            pl.BlockSpec((1, 1, 1, d), lambda i, j: (i, 1, 0, 0)),
            pl.BlockSpec((1, 1, 1, d), lambda i, j: (i, 0, 0, 0)),
            const((nq, d)), const((nq, d)), const((nq + F_ROWS, d)),
            const((F_ROWS, 1)),
            pl.BlockSpec((1, 16, t), lambda i, j: (i, 0, j)),
            pl.BlockSpec((1, 16, t), lambda i, j: (i, 0, j)),
            const((t, t)),
        ],
        out_specs=[
            pl.BlockSpec((1, 1, nk + nq, t), lambda i, j: (i, j, 0, 0)),
            pl.BlockSpec((1, t, nk), lambda i, j: (i, j, 0)),
            pl.BlockSpec((1, 2, BRANCH_WIDTH, t // 2), lambda i, j: (i, j, 0, 0)),
        ],
        out_shape=[
            jax.ShapeDtypeStruct((b, ns, nk + nq, t), BF16),
            jax.ShapeDtypeStruct((b, s, nk), BF16),
            jax.ShapeDtypeStruct((b, 2 * ns, BRANCH_WIDTH, t // 2), BF16),
        ],
        scratch_shapes=[pltpu.VMEM((F_ROWS, LANES), F32)],
        compiler_params=_params("arbitrary", "arbitrary"),
        name=f"projection_l{l}",
    )(x, sc, sh, wq, wk, wv, fb, cos, sin, u)


def _masked_q(q_ref, head, width):
    per_group = LANES // width
    g = head // per_group
    qg = q_ref[0, 0, LANES * g:LANES * (g + 1), :]
    rows = lax.broadcasted_iota(jnp.int32, qg.shape, 0)
    r0 = width * (head % per_group)
    return jnp.where((rows >= r0) & (rows < r0 + width), qg, jnp.zeros_like(qg))


def _load_k(k_ref, kb, group, t):
    k0 = pl.multiple_of(kb * t, t)
    return k_ref[0, pl.ds(k0, t), LANES * group:LANES * (group + 1)]


def _load_v(v_ref, kb, head):
    return v_ref[0, kb, HEAD_DIM * head:HEAD_DIM * (head + 1), :]


def _softmax_step(s, carry, vblk):
    m, acc = carry
    t = s.shape[1]
    m_new = jnp.maximum(m, jnp.max(s, axis=0, keepdims=True))
    p = jnp.exp2(s - m_new).astype(BF16)
    alpha = jnp.exp2(m - m_new)
    vaug = jnp.concatenate([vblk, jnp.ones((ONES_ROWS, t), BF16)], axis=0)
    return m_new, alpha * acc + _dot(vaug, p)


def _causal_tile(t, strict):
    key = lax.broadcasted_iota(jnp.int32, (t, t), 0)
    qry = lax.broadcasted_iota(jnp.int32, (t, t), 1)
    return key < qry if strict else key <= qry


def _staggered(n, score_fn, step_fn, first=None, before_last=None):
    out = []
    s_next = score_fn(0) if first is None else first
    for h in range(n):
        s = s_next
        if h + 1 < n:
            s_next = score_fn(h + 1)
        elif before_last is not None:
            before_last()
        out.append(step_fn(h, s))
    return tuple(out)


def _softmax_blocks(n, score, adjust, vblock, scratch, lo, qb, t):
    s0_ref, m_ref, acc_ref = scratch
    ones = jnp.ones((ONES_ROWS, t), BF16)

    def prefetch_before(kb):
        def prefetch():
            s0_ref[...] = score(jnp.maximum(kb - 1, 0), 0)
        return prefetch

    def first_step(h, s):
        s = adjust(qb, h, s, True)
        m = jnp.max(s, axis=0, keepdims=True)
        m_ref[h] = m
        acc_ref[h] = _dot(jnp.concatenate([vblock(qb, h), ones], axis=0), jnp.exp2(s - m).astype(BF16))

    _staggered(n, lambda h: score(qb, h), first_step, before_last=prefetch_before(qb))

    def loop_body(i, _):
        kb = qb - 1 - i
        excess = []

        def fast_step(h, s):
            m = m_ref[h]
            s = adjust(kb, h, s, False)
            excess.append(jnp.max(s, axis=0, keepdims=True) - m)
            return _dot(jnp.concatenate([vblock(kb, h), ones], axis=0), jnp.exp2(s - m).astype(BF16))

        deltas = _staggered(n, lambda h: score(kb, h), fast_step, first=s0_ref[...],
                            before_last=prefetch_before(kb))
        over = jnp.max(functools.reduce(jnp.maximum, excess)) > SOFTMAX_MARGIN

        @pl.when(jnp.logical_not(over))
        def _():
            for h in range(n):
                acc_ref[h] += deltas[h]

        @pl.when(over)
        def _():
            def exact_step(h, s):
                m_ref[h], acc_ref[h] = _softmax_step(adjust(kb, h, s, False), (m_ref[h], acc_ref[h]), vblock(kb, h))

            _staggered(n, lambda h: score(kb, h), exact_step)

        return 0

    lax.fori_loop(0, qb - lo, loop_body, 0)
    return [acc_ref[h, 0:HEAD_DIM] / acc_ref[h, HEAD_DIM:HEAD_DIM + 1] for h in range(n)]


def _diff_kernel(q_ref, k_ref, v_ref, lam_ref, g_ref, o_ref, *scratch, lambda_init):
    qb = pl.program_id(1)
    t = q_ref.shape[-1]
    n = 2 * N_HEADS
    per_group = LANES // DIFF_DIM
    lam_p = lam_ref[...]
    lam = (jnp.exp(jnp.sum(lam_p[0:1] * lam_p[1:2], keepdims=True))
           - jnp.exp(jnp.sum(lam_p[2:3] * lam_p[3:4], keepdims=True)) + lambda_init)
    qms = [_masked_q(q_ref, head, DIFF_DIM) for head in range(n)]

    def score(kb, head):
        return _dot(_load_k(k_ref, kb, head // per_group, t), qms[head])

    def adjust(kb, head, s, diag):
        return jnp.where(_causal_tile(t, False), s, NEG_BIG) if diag else s

    comps = _softmax_blocks(n, score, adjust, lambda kb, head: _load_v(v_ref, kb, head // 2), scratch, 0, qb, t)
    outs = []
    for hh in range(N_HEADS):
        o = comps[2 * hh] - lam * comps[2 * hh + 1]
        o = o * lax.rsqrt(jnp.mean(o * o, axis=0, keepdims=True) + LN_EPS)
        outs.append(o * (g_ref[...] * (1.0 - lambda_init)))
    o_ref[0] = jnp.concatenate(outs, axis=0).T.astype(BF16)


def _stick_kernel(q_ref, k_ref, v_ref, ut_ref, o_ref, run_ref, acc_ref):
    qb = pl.program_id(1)
    t = q_ref.shape[-1]
    sub = t // 2
    per_group = LANES // HEAD_DIM
    qms = [_masked_q(q_ref, head, HEAD_DIM) for head in range(N_HEADS)]
    run_ref[...] = jnp.zeros_like(run_ref)
    acc_ref[...] = jnp.zeros_like(acc_ref)

    def half_step(hb, q0, key_off):
        cols = slice(q0, t)
        k0 = pl.multiple_of(hb * sub, sub)
        kblks = [k_ref[0, pl.ds(k0, sub), LANES * g:LANES * (g + 1)] for g in range(N_HEADS // per_group)]
        if key_off is not None:
            key = lax.broadcasted_iota(jnp.int32, (sub, t - q0), 0) + key_off
            qry = lax.broadcasted_iota(jnp.int32, (sub, t - q0), 1) + q0
            valid = key < qry

        def step(head, z):
            lk = -(jnp.maximum(z, 0.0) + jnp.log(1.0 + jnp.exp(-jnp.abs(z))))
            lkm = lk if key_off is None else jnp.where(valid, lk, 0.0)
            hi = lkm.astype(BF16)
            lo = (lkm - hi.astype(F32)).astype(BF16)
            within = _dot(ut_ref[...], hi) + _dot(ut_ref[...], lo)
            a = jnp.exp(z + lk + within + run_ref[head, :, cols])
            if key_off is not None:
                a = jnp.where(valid, a, 0.0)
            acc_ref[head, :, cols] += _dot(v_ref[0, hb, HEAD_DIM * head:HEAD_DIM * (head + 1), :], a.astype(BF16))
            run_ref[head, :, cols] += within[0:1] + lkm[0:1]

        _staggered(N_HEADS, lambda head: _dot(kblks[head // per_group], qms[head][:, cols]), step)

    def alive():
        worst = functools.reduce(jnp.maximum, [run_ref[head] for head in range(N_HEADS)])
        return jnp.max(worst) > STICK_EXIT

    half_step(2 * qb + 1, sub, sub)
    half_step(2 * qb, 0, 0)

    def earlier(state):
        i, _ = state
        half_step(2 * qb - 1 - i, 0, None)
        return i + 1, alive()

    lax.while_loop(lambda st: jnp.logical_and(st[0] < 2 * qb, st[1]), earlier, (jnp.int32(0), alive()))
    o_ref[0] = jnp.concatenate([acc_ref[head] for head in range(N_HEADS)], axis=0).T.astype(BF16)


def _dilated_kernel(q_ref, k_ref, v_ref, bias_ref, o_ref, *scratch):
    qb = pl.program_id(1)
    t = q_ref.shape[-1]
    n_back = bias_ref.shape[0] - 1
    per_group = LANES // HEAD_DIM
    qms = [_masked_q(q_ref, head, HEAD_DIM) for head in range(N_HEADS)]

    def score(kb, head):
        return _dot(_load_k(k_ref, kb, head // per_group, t), qms[head])

    def adjust(kb, head, s, diag):
        return bias_ref[qb - kb] + s

    outs = _softmax_blocks(N_HEADS, score, adjust, lambda kb, head: _load_v(v_ref, kb, head), scratch,
                           jnp.maximum(qb - n_back, 0), qb, t)
    o_ref[0] = jnp.concatenate(outs, axis=0).T.astype(BF16)


def _forget_kernel(q_ref, k_ref, v_ref, o_ref, *scratch):
    qb = pl.program_id(1)
    t = q_ref.shape[-1]
    qas = [q_ref[0, 0, AUG * head:AUG * (head + 1), :] for head in range(N_HEADS)]

    def score(kb, head):
        return _dot(_load_k(k_ref, kb, head, t), qas[head])

    def adjust(kb, head, s, diag):
        return jnp.where(_causal_tile(t, False), s, NEG_BIG) if diag else s

    outs = _softmax_blocks(N_HEADS, score, adjust, lambda kb, head: _load_v(v_ref, kb, head), scratch, 0, qb, t)
    o_ref[0] = jnp.concatenate(outs, axis=0).T.astype(BF16)


def _dilated_bias(t):
    n_back = max(w for w, _ in DILATED_PATTERNS) // t
    kk = np.arange(t)[:, None]
    qq = np.arange(t)[None, :]
    tiles = []
    for d in range(n_back + 1):
        dist = d * t + qq - kk
        count = np.zeros((t, t), np.float64)
        for window, dil in DILATED_PATTERNS:
            count += (dist >= 0) & (dist % dil == 0) & (dist // dil <= window // dil)
        tiles.append(np.where(count > 0, np.log2(np.maximum(count, 1.0)), NEG_BIG))
    return jnp.asarray(np.stack(tiles), F32)


def _attention(kind, branch, qv, k, extra_inputs, extra_specs, kernel, l, scratch=(), v_half=None):
    b, ns, _, t = qv.shape
    s = ns * t
    n_aug = N_HEADS * AUG
    if branch == 3:
        width, qk_block = n_aug, 0
    else:
        width, qk_block = BRANCH_WIDTH, n_aug // BRANCH_WIDTH + branch
    v_block = (n_aug + 3 * BRANCH_WIDTH) // BRANCH_WIDTH + branch
    in_specs = [
        pl.BlockSpec((1, 1, width, t), lambda i, j: (i, j, qk_block, 0)),
        pl.BlockSpec((1, s, width), lambda i, j: (i, 0, qk_block)),
        pl.BlockSpec((1, ns, BRANCH_WIDTH, t), lambda i, j: (i, 0, v_block, 0)) if v_half is None else
        pl.BlockSpec((1, 2 * ns, BRANCH_WIDTH, t // 2), lambda i, j: (i, 0, 0, 0)),
    ] + extra_specs
    return pl.pallas_call(
        kernel,
        grid=(b, ns),
        in_specs=in_specs,
        out_specs=pl.BlockSpec((1, t, BRANCH_WIDTH), lambda i, j: (i, j, 0)),
        out_shape=jax.ShapeDtypeStruct((b, s, BRANCH_WIDTH), BF16),
        scratch_shapes=list(scratch),
        compiler_params=_params("arbitrary", "arbitrary"),
        name=f"{kind}_attention_l{l}",
    )(qv, k, qv if v_half is None else v_half, *extra_inputs)


def _merge_kernel(x_ref, sc_ref, sh_ref, ga_ref, oa_ref, ob_ref, oc_ref, od_ref, wg_ref, wb_ref, wo_ref,
                  lng_ref, lnb_ref, out_ref):
    x = x_ref[0]
    d = x.shape[-1]
    h = (x * (1.0 + sc_ref[0, 0]) + sh_ref[0, 0]).astype(BF16)
    merged = None
    for n, o_ref in enumerate((oa_ref, ob_ref, oc_ref, od_ref)):
        gate = 1.0 / (1.0 + jnp.exp(-_dot(h, wg_ref[:, d * n:d * (n + 1)])))
        y = gate * _dot(o_ref[0], wb_ref[n])
        merged = y if merged is None else merged + y
    mix = _dot(merged.astype(BF16), wo_ref[...])
    out_ref[0] = _layer_norm(DEEPNORM_ALPHA * x + ga_ref[0, 0] * mix, lng_ref[...], lnb_ref[...])


def _merge(x, mod, outs, wg, wb, wo, ln_g, ln_b, l):
    b, s, d = x.shape
    t = SEQ_TILE
    const = lambda shape: pl.BlockSpec(shape, lambda i, j: (0,) * len(shape))
    mod_spec = lambda idx: pl.BlockSpec((1, 1, 1, d), lambda i, j: (i, idx, 0, 0))
    o_spec = pl.BlockSpec((1, t, BRANCH_WIDTH), lambda i, j: (i, j, 0))
    return pl.pallas_call(
        _merge_kernel,
        grid=(b, s // t),
        in_specs=[pl.BlockSpec((1, t, d), lambda i, j: (i, j, 0)), mod_spec(1), mod_spec(0), mod_spec(2),
                  o_spec, o_spec, o_spec, o_spec,
                  const((d, N_BRANCHES * d)), const((N_BRANCHES, BRANCH_WIDTH, d)), const((d, d)),
                  const((1, d)), const((1, d))],
        out_specs=pl.BlockSpec((1, t, d), lambda i, j: (i, j, 0)),
        out_shape=jax.ShapeDtypeStruct((b, s, d), F32),
        compiler_params=_params("arbitrary", "arbitrary"),
        name=f"merge_l{l}",
    )(x, mod, mod, mod, *outs, wg, wb, wo, ln_g.reshape(1, d), ln_b.reshape(1, d))


def _ffn_kernel(x_ref, sc_ref, sh_ref, gf_ref, wup_ref, cw_ref, cb_ref, wdn_ref, lng_ref, lnb_ref,
                out_ref, tail_ref):
    t = x_ref.shape[1]
    chunk = D_FF // 2

    @pl.when(pl.program_id(1) == 0)
    def _():
        tail_ref[...] = jnp.zeros_like(tail_ref)

    x = x_ref[0]
    h = (x * (1.0 + sc_ref[0, 0]) + sh_ref[0, 0]).astype(BF16)
    rows = lax.broadcasted_iota(jnp.int32, (8, chunk), 0)

    def conv(col0):
        cols = slice(col0, col0 + chunk)
        u = _dot(h, wup_ref[:, cols])
        tail = tail_ref[:, cols]
        u1, u2 = pltpu.roll(u, 1, 0), pltpu.roll(u, 2, 0)
        head1 = jnp.where(rows < 1, pltpu.roll(tail, 1, 0), u1[0:8])
        head2 = jnp.where(rows < 2, pltpu.roll(tail, 2, 0), u2[0:8])
        u1 = jnp.concatenate([head1, u1[8:]], axis=0)
        u2 = jnp.concatenate([head2, u2[8:]], axis=0)
        tail_ref[:, cols] = u[t - 8:t]
        return cw_ref[2:3, cols] * u + cw_ref[1:2, cols] * u1 + cw_ref[0:1, cols] * u2 + cb_ref[:, cols]

    ffn = None
    for c in range(D_FF // chunk):
        a = conv(c * chunk)
        g = conv(D_FF + c * chunk)
        act = (a / (1.0 + jnp.exp(-a)) * g).astype(BF16)
        y = _dot(act, wdn_ref[c * chunk:(c + 1) * chunk, :])
        ffn = y if ffn is None else ffn + y
    out_ref[0] = _layer_norm(DEEPNORM_ALPHA * x + gf_ref[0, 0] * ffn, lng_ref[...], lnb_ref[...])


def _conv_ffn(x, mod, wup, cw, cb, wdn, ln_g, ln_b, l):
    b, s, d = x.shape
    t = SEQ_TILE
    const = lambda shape: pl.BlockSpec(shape, lambda i, j: (0,) * len(shape))
    mod_spec = lambda idx: pl.BlockSpec((1, 1, 1, d), lambda i, j: (i, idx, 0, 0))
    return pl.pallas_call(
        _ffn_kernel,
        grid=(b, s // t),
        in_specs=[pl.BlockSpec((1, t, d), lambda i, j: (i, j, 0)), mod_spec(4), mod_spec(3), mod_spec(5),
                  const((d, 2 * D_FF)), const((CONV_WIDTH, 2 * D_FF)), const((1, 2 * D_FF)), const((D_FF, d)),
                  const((1, d)), const((1, d))],
        out_specs=pl.BlockSpec((1, t, d), lambda i, j: (i, j, 0)),
        out_shape=jax.ShapeDtypeStruct((b, s, d), F32),
        scratch_shapes=[pltpu.VMEM((8, 2 * D_FF), F32)],
        compiler_params=_params("arbitrary", "arbitrary"),
        name=f"conv_ffn_l{l}",
    )(x, mod, mod, mod, wup, cw, cb.reshape(1, 2 * D_FF), wdn, ln_g.reshape(1, d), ln_b.reshape(1, d))


def kernel(x, c, positions, w_ada, b_ada, w_in, lam_q1, lam_k1, lam_q2, lam_k2, subln_g, forget_b, w_branch,
           w_o, ln1_g, ln1_b, w_up, conv_w, conv_b, w_down, ln2_g, ln2_b):
    b, s, d = x.shape
    t = SEQ_TILE
    bw = BRANCH_WIDTH
    const = lambda shape: pl.BlockSpec(shape, lambda i, j: (0,) * len(shape))

    mod = _modulation(c, w_ada, b_ada)
    cos, sin = _rope_tables(positions)
    bias = _dilated_bias(t)
    sub = t // 2
    ut = jnp.asarray(np.triu(np.ones((sub, sub), np.float32), 1), BF16)
    chains = lambda n: [pltpu.VMEM((t, t), F32), pltpu.VMEM((n, 1, t), F32),
                        pltpu.VMEM((n, HEAD_DIM + ONES_ROWS, t), F32)]

    for l in range(DEPTH):
        lambda_init = 0.8 - 0.6 * math.exp(-0.3 * l)
        w = w_in[l]
        col = lambda i: w[:, bw * i:bw * (i + 1)]
        wq = jnp.concatenate([col(0) * (DIFF_DIM ** -0.5 * LOG2E), col(3) * HEAD_DIM ** -0.5,
                              col(6) * (HEAD_DIM ** -0.5 * LOG2E), col(9) * (HEAD_DIM ** -0.5 * LOG2E)],
                             axis=1).T.astype(BF16)
        wk = jnp.concatenate([col(1), col(4), col(7), col(10)], axis=1).T.astype(BF16)
        wf = jnp.zeros((d, F_ROWS), F32).at[:, :N_HEADS].set(w[:, 12 * bw:12 * bw + N_HEADS])
        wv = jnp.concatenate([col(2), col(5), col(8), col(11), wf], axis=1).T.astype(BF16)
        wg = w[:, 12 * bw + N_HEADS:].astype(BF16)
        fb = jnp.zeros((F_ROWS, 1), F32).at[:N_HEADS, 0].set(forget_b[l])

        qv, k, vs = _projection(x, mod[l], mod[l], wq, wk, wv, fb, cos, sin, l)

        lam = jnp.stack([lam_q1[l], lam_k1[l], lam_q2[l], lam_k2[l]])
        oa = _attention("diff", 0, qv, k, [lam, subln_g[l].reshape(HEAD_DIM, 1)],
                        [const((4, DIFF_DIM)), const((HEAD_DIM, 1))],
                        functools.partial(_diff_kernel, lambda_init=lambda_init), l, chains(2 * N_HEADS))
        ob = _attention("stick", 1, qv, k, [ut], [const((sub, sub))], _stick_kernel, l,
                        [pltpu.VMEM((N_HEADS, 1, t), F32), pltpu.VMEM((N_HEADS, HEAD_DIM, t), F32)], vs)
        oc = _attention("dilated", 2, qv, k, [bias], [const(bias.shape)], _dilated_kernel, l, chains(N_HEADS))
        od = _attention("forget", 3, qv, k, [], [], _forget_kernel, l, chains(N_HEADS))

        x = _merge(x, mod[l], (oa, ob, oc, od), wg, w_branch[l].astype(BF16), w_o[l].astype(BF16),
                   ln1_g[l], ln1_b[l], l)
        x = _conv_ffn(x, mod[l], w_up[l].astype(BF16), conv_w[l], conv_b[l], w_down[l].astype(BF16),
                      ln2_g[l], ln2_b[l], l)
    return x
```

```python
import functools
import math

import numpy as np
import jax
import jax.numpy as jnp
from jax import lax
from jax.experimental import pallas as pl
from jax.experimental.pallas import tpu as pltpu

D_MODEL = 1024
DEPTH = 2
HEAD_DIM = 64
DIFF_DIM = HEAD_DIM // 2
N_HEADS = 4
BRANCH_WIDTH = N_HEADS * HEAD_DIM
N_BRANCHES = 4
ROPE_THETA = 500000.0
ROPE_FRACTION = 4
DILATED_PATTERNS = ((128, 1), (512, 4), (2048, 16))
D_FF = 2048
CONV_WIDTH = 3
LN_EPS = 1e-5
DEEPNORM_ALPHA = (2.0 * DEPTH) ** 0.25

SEQ_TILE = 512
LANES = 128
F_ROWS = 16
ONES_ROWS = 16
NEG_BIG = -1e30
LOG2E = math.log2(math.e)
SOFTMAX_MARGIN = -1e30
STICK_EXIT = -105.0
AUG = LANES
VMEM_LIMIT = 56 * 2**20

F32 = jnp.float32
BF16 = jnp.bfloat16
_NT = (((1,), (1,)), ((), ()))


def _dot(a, b):
    return jnp.dot(a, b, preferred_element_type=F32)


def _params(*sem):
    return pltpu.CompilerParams(dimension_semantics=sem, vmem_limit_bytes=VMEM_LIMIT)


def _split3(x):
    hi = x.astype(BF16)
    r1 = x - hi.astype(F32)
    mid = r1.astype(BF16)
    lo = (r1 - mid.astype(F32)).astype(BF16)
    return hi, mid, lo


def _layer_norm(r, g, b):
    mu = jnp.mean(r, axis=-1, keepdims=True)
    d = r - mu
    var = jnp.mean(d * d, axis=-1, keepdims=True)
    return d * lax.rsqrt(var + LN_EPS) * g + b


def _mod_kernel(c_ref, w_ref, b_ref, o_ref):
    c = c_ref[...]
    a = c / (1.0 + jnp.exp(-c))
    a_hi = a.astype(BF16)
    a_lo = (a - a_hi.astype(F32)).astype(BF16)
    w = w_ref[0]
    w_hi = w.astype(BF16)
    w_lo = (w - w_hi.astype(F32)).astype(BF16)
    o_ref[0] = _dot(a_hi, w_hi) + _dot(a_lo, w_hi) + _dot(a_hi, w_lo) + b_ref[0]


def _modulation(c, w_ada, b_ada):
    b = c.shape[0]
    rows = 8
    c_pad = jnp.zeros((rows, D_MODEL), F32).at[:b].set(c)
    out = pl.pallas_call(
        _mod_kernel,
        grid=(DEPTH, 6),
        in_specs=[
            pl.BlockSpec((rows, D_MODEL), lambda l, j: (0, 0)),
            pl.BlockSpec((1, D_MODEL, D_MODEL), lambda l, j: (l, 0, j)),
            pl.BlockSpec((1, 1, D_MODEL), lambda l, j: (l, 0, j)),
        ],
        out_specs=pl.BlockSpec((1, rows, D_MODEL), lambda l, j: (l, 0, j)),
        out_shape=jax.ShapeDtypeStruct((DEPTH, rows, 6 * D_MODEL), F32),
        compiler_params=_params("arbitrary", "arbitrary"),
        name="adaln_mod",
    )(c_pad, w_ada, b_ada.reshape(DEPTH, 1, 6 * D_MODEL))
    return out[:, :b].reshape(DEPTH, b, 6, 1, D_MODEL)


def _rope_kernel(pos_ref, invf_ref, sgn_ref, cos_ref, sin_ref):
    ang = invf_ref[...] * pos_ref[0].astype(F32)
    cos_ref[0] = jnp.cos(ang)
    sin_ref[0] = jnp.sin(ang) * sgn_ref[...]


def _rope_tables(positions):
    b, s = positions.shape
    r_a, r_c = DIFF_DIM // ROPE_FRACTION, HEAD_DIM // ROPE_FRACTION
    inv_a = jnp.power(jnp.float32(ROPE_THETA), -2.0 * jnp.arange(r_a // 2, dtype=F32) / r_a)
    inv_c = jnp.power(jnp.float32(ROPE_THETA), -2.0 * jnp.arange(r_c // 2, dtype=F32) / r_c)
    invf = jnp.concatenate([inv_a, inv_a, inv_c]).reshape(16, 1)
    sgn = jnp.concatenate([-jnp.ones(4, F32), jnp.ones(12, F32)]).reshape(16, 1)
    return pl.pallas_call(
        _rope_kernel,
        grid=(b,),
        in_specs=[
            pl.BlockSpec((1, 1, s), lambda i: (i, 0, 0)),
            pl.BlockSpec((16, 1), lambda i: (0, 0)),
            pl.BlockSpec((16, 1), lambda i: (0, 0)),
        ],
        out_specs=[pl.BlockSpec((1, 16, s), lambda i: (i, 0, 0))] * 2,
        out_shape=[jax.ShapeDtypeStruct((b, 16, s), F32)] * 2,
        compiler_params=_params("arbitrary"),
        name="rope_tables",
    )(positions.reshape(b, 1, s), invf, sgn)


def _rope_rows(y, cos, sin):
    cos_a, sin_a, cos_c, sin_c = cos[0:8], sin[0:8], cos[8:16], sin[8:16]
    parts = []
    for hh in range(2 * N_HEADS):
        r0 = DIFF_DIM * hh
        seg = y[r0:r0 + 8]
        parts.append(seg * cos_a + pltpu.roll(seg, 4, 0) * sin_a)
        parts.append(y[r0 + 8:r0 + DIFF_DIM])
    parts.append(y[BRANCH_WIDTH:2 * BRANCH_WIDTH])
    for hh in range(N_HEADS):
        r0 = 2 * BRANCH_WIDTH + HEAD_DIM * hh
        t0, t1 = y[r0:r0 + 8], y[r0 + 8:r0 + 16]
        parts.append(t0 * cos_c - t1 * sin_c)
        parts.append(t1 * cos_c + t0 * sin_c)
        parts.append(y[r0 + 16:r0 + HEAD_DIM])
    parts.append(y[3 * BRANCH_WIDTH:4 * BRANCH_WIDTH])
    return jnp.concatenate(parts, axis=0)


def _proj_kernel(x_ref, sc_ref, sh_ref, wq_ref, wk_ref, wv_ref, fb_ref, cos_ref, sin_ref, u_ref,
                 qv_ref, k_ref, vs_ref, carry_ref):
    t = x_ref.shape[1]
    nq = 4 * BRANCH_WIDTH
    n3 = 3 * BRANCH_WIDTH

    @pl.when(pl.program_id(1) == 0)
    def _():
        carry_ref[...] = jnp.zeros_like(carry_ref)

    h = (x_ref[0] * (1.0 + sc_ref[0, 0]) + sh_ref[0, 0]).astype(BF16)
    cos, sin = cos_ref[0], sin_ref[0]

    yq = _rope_rows(lax.dot_general(wq_ref[...], h, _NT, preferred_element_type=F32), cos, sin)
    yk = _rope_rows(lax.dot_general(wk_ref[...], h, _NT, preferred_element_type=F32), cos, sin)
    yv = lax.dot_general(wv_ref[...], h, _NT, preferred_element_type=F32)

    fl = yv[nq:nq + F_ROWS] + fb_ref[...]
    lf = jnp.minimum(fl, 0.0) - jnp.log(1.0 + jnp.exp(-jnp.abs(fl)))
    cs = _dot(jnp.concatenate(_split3(lf), axis=0), u_ref[...])
    fcum = cs[0:F_ROWS] + cs[F_ROWS:2 * F_ROWS] + cs[2 * F_ROWS:3 * F_ROWS] + carry_ref[:, 0:1]
    carry_ref[...] = jnp.broadcast_to(fcum[:, t - 1:t], carry_ref.shape)

    f2 = fcum * LOG2E
    row8 = lax.broadcasted_iota(jnp.int32, (8, t), 0)
    ones3 = jnp.where(row8 < 3, 1.0, 0.0)
    pad = jnp.zeros((AUG - HEAD_DIM - 16, t), F32)
    q_rows, k_rows = [], []
    for hh in range(N_HEADS):
        hi, mid, lo = (p.astype(F32) for p in _split3(f2[hh:hh + 1]))
        pieces = jnp.where(row8 == 0, hi, jnp.where(row8 == 1, mid, jnp.where(row8 == 2, lo, 0.0)))
        feat = slice(n3 + HEAD_DIM * hh, n3 + HEAD_DIM * (hh + 1))
        q_rows += [yq[feat], ones3, pieces, pad]
        k_rows += [yk[feat], -pieces, ones3, pad]
    qv_ref[0, 0, 0:N_HEADS * AUG + n3, :] = jnp.concatenate(q_rows + [yq[0:n3]], axis=0).astype(BF16)
    qv_ref[0, 0, N_HEADS * AUG + n3:N_HEADS * AUG + n3 + nq, :] = yv[0:nq].astype(BF16)
    for half in range(2):
        vs_ref[0, half] = yv[BRANCH_WIDTH:2 * BRANCH_WIDTH, half * (t // 2):(half + 1) * (t // 2)].astype(BF16)
    k_ref[0] = jnp.concatenate(k_rows + [yk[0:n3]], axis=0).T.astype(BF16)


def _projection(x, sc, sh, wq, wk, wv, fb, cos, sin, l):
    b, s, d = x.shape
    t = SEQ_TILE
    ns = s // t
    nq = 4 * BRANCH_WIDTH
    nk = N_HEADS * AUG + 3 * BRANCH_WIDTH
    u = jnp.asarray(np.triu(np.ones((t, t), np.float32)), BF16)
    const = lambda shape: pl.BlockSpec(shape, lambda i, j: (0,) * len(shape))
    return pl.pallas_call(
        _proj_kernel,
        grid=(b, ns),
        in_specs=[
            pl.BlockSpec((1, t, d), lambda i, j: (i, j, 0)),
            pl.BlockSpec((1, 1, 1, d), lambda i, j: (i, 1, 0, 0)),
            pl.BlockSpec((1, 1, 1, d), lambda i, j: (i, 0, 0, 0)),
            const((nq, d)), const((nq, d)), const((nq + F_ROWS, d)),
            const((F_ROWS, 1)),
            pl.BlockSpec((1, 16, t), lambda i, j: (i, 0, j)),
            pl.BlockSpec((1, 16, t), lambda i, j: (i, 0, j)),
            const((t, t)),
        ],
        out_specs=[
            pl.BlockSpec((1, 1, nk + nq, t), lambda i, j: (i, j, 0, 0)),
            pl.BlockSpec((1, t, nk), lambda i, j: (i, j, 0)),
            pl.BlockSpec((1, 2, BRANCH_WIDTH, t // 2), lambda i, j: (i, j, 0, 0)),
        ],
        out_shape=[
            jax.ShapeDtypeStruct((b, ns, nk + nq, t), BF16),
            jax.ShapeDtypeStruct((b, s, nk), BF16),
            jax.ShapeDtypeStruct((b, 2 * ns, BRANCH_WIDTH, t // 2), BF16),
        ],
        scratch_shapes=[pltpu.VMEM((F_ROWS, LANES), F32)],
        compiler_params=_params("arbitrary", "arbitrary"),
        name=f"projection_l{l}",
    )(x, sc, sh, wq, wk, wv, fb, cos, sin, u)


def _masked_q(q_ref, head, width):
    per_group = LANES // width
    g = head // per_group
    qg = q_ref[0, 0, LANES * g:LANES * (g + 1), :]
    rows = lax.broadcasted_iota(jnp.int32, qg.shape, 0)
    r0 = width * (head % per_group)
    return jnp.where((rows >= r0) & (rows < r0 + width), qg, jnp.zeros_like(qg))


def _load_k(k_ref, kb, group, t):
    k0 = pl.multiple_of(kb * t, t)
    return k_ref[0, pl.ds(k0, t), LANES * group:LANES * (group + 1)]


def _load_v(v_ref, kb, head):
    return v_ref[0, kb, HEAD_DIM * head:HEAD_DIM * (head + 1), :]


def _softmax_step(s, carry, vblk):
    m, acc = carry
    t = s.shape[1]
    m_new = jnp.maximum(m, jnp.max(s, axis=0, keepdims=True))
    p = jnp.exp2(s - m_new).astype(BF16)
    alpha = jnp.exp2(m - m_new)
    vaug = jnp.concatenate([vblk, jnp.ones((ONES_ROWS, t), BF16)], axis=0)
    return m_new, alpha * acc + _dot(vaug, p)


def _causal_tile(t, strict):
    key = lax.broadcasted_iota(jnp.int32, (t, t), 0)
    qry = lax.broadcasted_iota(jnp.int32, (t, t), 1)
    return key < qry if strict else key <= qry


def _staggered(n, score_fn, step_fn, first=None, before_last=None):
    out = []
    s_next = score_fn(0) if first is None else first
    for h in range(n):
        s = s_next
        if h + 1 < n:
            s_next = score_fn(h + 1)
        elif before_last is not None:
            before_last()
        out.append(step_fn(h, s))
    return tuple(out)


def _softmax_blocks(n, score, adjust, vblock, scratch, lo, qb, t):
    s0_ref, m_ref, acc_ref = scratch
    ones = jnp.ones((ONES_ROWS, t), BF16)

    def prefetch_before(kb):
        def prefetch():
            s0_ref[...] = score(jnp.maximum(kb - 1, 0), 0)
        return prefetch

    def first_step(h, s):
        s = adjust(qb, h, s, True)
        m = jnp.max(s, axis=0, keepdims=True)
        m_ref[h] = m
        acc_ref[h] = _dot(jnp.concatenate([vblock(qb, h), ones], axis=0), jnp.exp2(s - m).astype(BF16))

    _staggered(n, lambda h: score(qb, h), first_step, before_last=prefetch_before(qb))

    def loop_body(i, _):
        kb = qb - 1 - i
        excess = []

        def fast_step(h, s):
            m = m_ref[h]
            s = adjust(kb, h, s, False)
            excess.append(jnp.max(s, axis=0, keepdims=True) - m)
            return _dot(jnp.concatenate([vblock(kb, h), ones], axis=0), jnp.exp2(s - m).astype(BF16))

        deltas = _staggered(n, lambda h: score(kb, h), fast_step, first=s0_ref[...],
                            before_last=prefetch_before(kb))
        over = jnp.max(functools.reduce(jnp.maximum, excess)) > SOFTMAX_MARGIN

        @pl.when(jnp.logical_not(over))
        def _():
            for h in range(n):
                acc_ref[h] += deltas[h]

        @pl.when(over)
        def _():
            def exact_step(h, s):
                m_ref[h], acc_ref[h] = _softmax_step(adjust(kb, h, s, False), (m_ref[h], acc_ref[h]), vblock(kb, h))

            _staggered(n, lambda h: score(kb, h), exact_step)

        return 0

    lax.fori_loop(0, qb - lo, loop_body, 0)
    return [acc_ref[h, 0:HEAD_DIM] / acc_ref[h, HEAD_DIM:HEAD_DIM + 1] for h in range(n)]


def _diff_kernel(q_ref, k_ref, v_ref, lam_ref, g_ref, o_ref, *scratch, lambda_init):
    qb = pl.program_id(1)
    t = q_ref.shape[-1]
    n = 2 * N_HEADS
    per_group = LANES // DIFF_DIM
    lam_p = lam_ref[...]
    lam = (jnp.exp(jnp.sum(lam_p[0:1] * lam_p[1:2], keepdims=True))
           - jnp.exp(jnp.sum(lam_p[2:3] * lam_p[3:4], keepdims=True)) + lambda_init)
    qms = [_masked_q(q_ref, head, DIFF_DIM) for head in range(n)]

    def score(kb, head):
        return _dot(_load_k(k_ref, kb, head // per_group, t), qms[head])

    def adjust(kb, head, s, diag):
        return jnp.where(_causal_tile(t, False), s, NEG_BIG) if diag else s

    comps = _softmax_blocks(n, score, adjust, lambda kb, head: _load_v(v_ref, kb, head // 2), scratch, 0, qb, t)
    outs = []
    for hh in range(N_HEADS):
        o = comps[2 * hh] - lam * comps[2 * hh + 1]
        o = o * lax.rsqrt(jnp.mean(o * o, axis=0, keepdims=True) + LN_EPS)
        outs.append(o * (g_ref[...] * (1.0 - lambda_init)))
    o_ref[0] = jnp.concatenate(outs, axis=0).T.astype(BF16)


def _stick_kernel(q_ref, k_ref, v_ref, ut_ref, o_ref, run_ref, acc_ref):
    qb = pl.program_id(1)
    t = q_ref.shape[-1]
    sub = t // 2
    per_group = LANES // HEAD_DIM
    qms = [_masked_q(q_ref, head, HEAD_DIM) for head in range(N_HEADS)]
    run_ref[...] = jnp.zeros_like(run_ref)
    acc_ref[...] = jnp.zeros_like(acc_ref)

    def half_step(hb, q0, key_off):
        cols = slice(q0, t)
        k0 = pl.multiple_of(hb * sub, sub)
        kblks = [k_ref[0, pl.ds(k0, sub), LANES * g:LANES * (g + 1)] for g in range(N_HEADS // per_group)]
        if key_off is not None:
            key = lax.broadcasted_iota(jnp.int32, (sub, t - q0), 0) + key_off
            qry = lax.broadcasted_iota(jnp.int32, (sub, t - q0), 1) + q0
            valid = key < qry

        def step(head, z):
            lk = -(jnp.maximum(z, 0.0) + jnp.log(1.0 + jnp.exp(-jnp.abs(z))))
            lkm = lk if key_off is None else jnp.where(valid, lk, 0.0)
            hi = lkm.astype(BF16)
            lo = (lkm - hi.astype(F32)).astype(BF16)
            within = _dot(ut_ref[...], hi) + _dot(ut_ref[...], lo)
            a = jnp.exp(z + lk + within + run_ref[head, :, cols])
            if key_off is not None:
                a = jnp.where(valid, a, 0.0)
            acc_ref[head, :, cols] += _dot(v_ref[0, hb, HEAD_DIM * head:HEAD_DIM * (head + 1), :], a.astype(BF16))
            run_ref[head, :, cols] += within[0:1] + lkm[0:1]

        _staggered(N_HEADS, lambda head: _dot(kblks[head // per_group], qms[head][:, cols]), step)

    def alive():
        worst = functools.reduce(jnp.maximum, [run_ref[head] for head in range(N_HEADS)])
        return jnp.max(worst) > STICK_EXIT

    half_step(2 * qb + 1, sub, sub)
    half_step(2 * qb, 0, 0)

    def earlier(state):
        i, _ = state
        half_step(2 * qb - 1 - i, 0, None)
        return i + 1, alive()

    lax.while_loop(lambda st: jnp.logical_and(st[0] < 2 * qb, st[1]), earlier, (jnp.int32(0), alive()))
    o_ref[0] = jnp.concatenate([acc_ref[head] for head in range(N_HEADS)], axis=0).T.astype(BF16)


def _dilated_kernel(q_ref, k_ref, v_ref, bias_ref, o_ref, *scratch):
    qb = pl.program_id(1)
    t = q_ref.shape[-1]
    n_back = bias_ref.shape[0] - 1
    per_group = LANES // HEAD_DIM
    qms = [_masked_q(q_ref, head, HEAD_DIM) for head in range(N_HEADS)]

    def score(kb, head):
        return _dot(_load_k(k_ref, kb, head // per_group, t), qms[head])

    def adjust(kb, head, s, diag):
        return bias_ref[qb - kb] + s

    outs = _softmax_blocks(N_HEADS, score, adjust, lambda kb, head: _load_v(v_ref, kb, head), scratch,
                           jnp.maximum(qb - n_back, 0), qb, t)
    o_ref[0] = jnp.concatenate(outs, axis=0).T.astype(BF16)


def _forget_kernel(q_ref, k_ref, v_ref, o_ref, *scratch):
    qb = pl.program_id(1)
    t = q_ref.shape[-1]
    qas = [q_ref[0, 0, AUG * head:AUG * (head + 1), :] for head in range(N_HEADS)]

    def score(kb, head):
        return _dot(_load_k(k_ref, kb, head, t), qas[head])

    def adjust(kb, head, s, diag):
        return jnp.where(_causal_tile(t, False), s, NEG_BIG) if diag else s

    outs = _softmax_blocks(N_HEADS, score, adjust, lambda kb, head: _load_v(v_ref, kb, head), scratch, 0, qb, t)
    o_ref[0] = jnp.concatenate(outs, axis=0).T.astype(BF16)


def _dilated_bias(t):
    n_back = max(w for w, _ in DILATED_PATTERNS) // t
    kk = np.arange(t)[:, None]
    qq = np.arange(t)[None, :]
    tiles = []
    for d in range(n_back + 1):
        dist = d * t + qq - kk
        count = np.zeros((t, t), np.float64)
        for window, dil in DILATED_PATTERNS:
            count += (dist >= 0) & (dist % dil == 0) & (dist // dil <= window // dil)
        tiles.append(np.where(count > 0, np.log2(np.maximum(count, 1.0)), NEG_BIG))
    return jnp.asarray(np.stack(tiles), F32)


def _attention(kind, branch, qv, k, extra_inputs, extra_specs, kernel, l, scratch=(), v_half=None):
    b, ns, _, t = qv.shape
    s = ns * t
    n_aug = N_HEADS * AUG
    if branch == 3:
        width, qk_block = n_aug, 0
    else:
        width, qk_block = BRANCH_WIDTH, n_aug // BRANCH_WIDTH + branch
    v_block = (n_aug + 3 * BRANCH_WIDTH) // BRANCH_WIDTH + branch
    in_specs = [
        pl.BlockSpec((1, 1, width, t), lambda i, j: (i, j, qk_block, 0)),
        pl.BlockSpec((1, s, width), lambda i, j: (i, 0, qk_block)),
        pl.BlockSpec((1, ns, BRANCH_WIDTH, t), lambda i, j: (i, 0, v_block, 0)) if v_half is None else
        pl.BlockSpec((1, 2 * ns, BRANCH_WIDTH, t // 2), lambda i, j: (i, 0, 0, 0)),
    ] + extra_specs
    return pl.pallas_call(
        kernel,
        grid=(b, ns),
        in_specs=in_specs,
        out_specs=pl.BlockSpec((1, t, BRANCH_WIDTH), lambda i, j: (i, j, 0)),
        out_shape=jax.ShapeDtypeStruct((b, s, BRANCH_WIDTH), BF16),
        scratch_shapes=list(scratch),
        compiler_params=_params("arbitrary", "arbitrary"),
        name=f"{kind}_attention_l{l}",
    )(qv, k, qv if v_half is None else v_half, *extra_inputs)


def _merge_kernel(x_ref, sc_ref, sh_ref, ga_ref, oa_ref, ob_ref, oc_ref, od_ref, wg_ref, wb_ref, wo_ref,
                  lng_ref, lnb_ref, out_ref):
    x = x_ref[0]
    d = x.shape[-1]
    h = (x * (1.0 + sc_ref[0, 0]) + sh_ref[0, 0]).astype(BF16)
    merged = None
    for n, o_ref in enumerate((oa_ref, ob_ref, oc_ref, od_ref)):
        gate = 1.0 / (1.0 + jnp.exp(-_dot(h, wg_ref[:, d * n:d * (n + 1)])))
        y = gate * _dot(o_ref[0], wb_ref[n])
        merged = y if merged is None else merged + y
    mix = _dot(merged.astype(BF16), wo_ref[...])
    out_ref[0] = _layer_norm(DEEPNORM_ALPHA * x + ga_ref[0, 0] * mix, lng_ref[...], lnb_ref[...])


def _merge(x, mod, outs, wg, wb, wo, ln_g, ln_b, l):
    b, s, d = x.shape
    t = SEQ_TILE
    const = lambda shape: pl.BlockSpec(shape, lambda i, j: (0,) * len(shape))
    mod_spec = lambda idx: pl.BlockSpec((1, 1, 1, d), lambda i, j: (i, idx, 0, 0))
    o_spec = pl.BlockSpec((1, t, BRANCH_WIDTH), lambda i, j: (i, j, 0))
    return pl.pallas_call(
        _merge_kernel,
        grid=(b, s // t),
        in_specs=[pl.BlockSpec((1, t, d), lambda i, j: (i, j, 0)), mod_spec(1), mod_spec(0), mod_spec(2),
                  o_spec, o_spec, o_spec, o_spec,
                  const((d, N_BRANCHES * d)), const((N_BRANCHES, BRANCH_WIDTH, d)), const((d, d)),
                  const((1, d)), const((1, d))],
        out_specs=pl.BlockSpec((1, t, d), lambda i, j: (i, j, 0)),
        out_shape=jax.ShapeDtypeStruct((b, s, d), F32),
        compiler_params=_params("arbitrary", "arbitrary"),
        name=f"merge_l{l}",
    )(x, mod, mod, mod, *outs, wg, wb, wo, ln_g.reshape(1, d), ln_b.reshape(1, d))


def _ffn_kernel(x_ref, sc_ref, sh_ref, gf_ref, wup_ref, cw_ref, cb_ref, wdn_ref, lng_ref, lnb_ref,
                out_ref, tail_ref):
    t = x_ref.shape[1]
    chunk = D_FF // 2

    @pl.when(pl.program_id(1) == 0)
    def _():
        tail_ref[...] = jnp.zeros_like(tail_ref)

    x = x_ref[0]
    h = (x * (1.0 + sc_ref[0, 0]) + sh_ref[0, 0]).astype(BF16)
    rows = lax.broadcasted_iota(jnp.int32, (8, chunk), 0)

    def conv(col0):
        cols = slice(col0, col0 + chunk)
        u = _dot(h, wup_ref[:, cols])
        tail = tail_ref[:, cols]
        u1, u2 = pltpu.roll(u, 1, 0), pltpu.roll(u, 2, 0)
        head1 = jnp.where(rows < 1, pltpu.roll(tail, 1, 0), u1[0:8])
        head2 = jnp.where(rows < 2, pltpu.roll(tail, 2, 0), u2[0:8])
        u1 = jnp.concatenate([head1, u1[8:]], axis=0)
        u2 = jnp.concatenate([head2, u2[8:]], axis=0)
        tail_ref[:, cols] = u[t - 8:t]
        return cw_ref[2:3, cols] * u + cw_ref[1:2, cols] * u1 + cw_ref[0:1, cols] * u2 + cb_ref[:, cols]

    ffn = None
    for c in range(D_FF // chunk):
        a = conv(c * chunk)
        g = conv(D_FF + c * chunk)
        act = (a / (1.0 + jnp.exp(-a)) * g).astype(BF16)
        y = _dot(act, wdn_ref[c * chunk:(c + 1) * chunk, :])
        ffn = y if ffn is None else ffn + y
    out_ref[0] = _layer_norm(DEEPNORM_ALPHA * x + gf_ref[0, 0] * ffn, lng_ref[...], lnb_ref[...])


def _conv_ffn(x, mod, wup, cw, cb, wdn, ln_g, ln_b, l):
    b, s, d = x.shape
    t = SEQ_TILE
    const = lambda shape: pl.BlockSpec(shape, lambda i, j: (0,) * len(shape))
    mod_spec = lambda idx: pl.BlockSpec((1, 1, 1, d), lambda i, j: (i, idx, 0, 0))
    return pl.pallas_call(
        _ffn_kernel,
        grid=(b, s // t),
        in_specs=[pl.BlockSpec((1, t, d), lambda i, j: (i, j, 0)), mod_spec(4), mod_spec(3), mod_spec(5),
                  const((d, 2 * D_FF)), const((CONV_WIDTH, 2 * D_FF)), const((1, 2 * D_FF)), const((D_FF, d)),
                  const((1, d)), const((1, d))],
        out_specs=pl.BlockSpec((1, t, d), lambda i, j: (i, j, 0)),
        out_shape=jax.ShapeDtypeStruct((b, s, d), F32),
        scratch_shapes=[pltpu.VMEM((8, 2 * D_FF), F32)],
        compiler_params=_params("arbitrary", "arbitrary"),
        name=f"conv_ffn_l{l}",
    )(x, mod, mod, mod, wup, cw, cb.reshape(1, 2 * D_FF), wdn, ln_g.reshape(1, d), ln_b.reshape(1, d))


def kernel(x, c, positions, w_ada, b_ada, w_in, lam_q1, lam_k1, lam_q2, lam_k2, subln_g, forget_b, w_branch,
           w_o, ln1_g, ln1_b, w_up, conv_w, conv_b, w_down, ln2_g, ln2_b):
    b, s, d = x.shape
    t = SEQ_TILE
    bw = BRANCH_WIDTH
    const = lambda shape: pl.BlockSpec(shape, lambda i, j: (0,) * len(shape))

    mod = _modulation(c, w_ada, b_ada)
    cos, sin = _rope_tables(positions)
    bias = _dilated_bias(t)
    sub = t // 2
    ut = jnp.asarray(np.triu(np.ones((sub, sub), np.float32), 1), BF16)
    chains = lambda n: [pltpu.VMEM((t, t), F32), pltpu.VMEM((n, 1, t), F32),
                        pltpu.VMEM((n, HEAD_DIM + ONES_ROWS, t), F32)]

    for l in range(DEPTH):
        lambda_init = 0.8 - 0.6 * math.exp(-0.3 * l)
        w = w_in[l]
        col = lambda i: w[:, bw * i:bw * (i + 1)]
        wq = jnp.concatenate([col(0) * (DIFF_DIM ** -0.5 * LOG2E), col(3) * HEAD_DIM ** -0.5,
                              col(6) * (HEAD_DIM ** -0.5 * LOG2E), col(9) * (HEAD_DIM ** -0.5 * LOG2E)],
                             axis=1).T.astype(BF16)
        wk = jnp.concatenate([col(1), col(4), col(7), col(10)], axis=1).T.astype(BF16)
        wf = jnp.zeros((d, F_ROWS), F32).at[:, :N_HEADS].set(w[:, 12 * bw:12 * bw + N_HEADS])
        wv = jnp.concatenate([col(2), col(5), col(8), col(11), wf], axis=1).T.astype(BF16)
        wg = w[:, 12 * bw + N_HEADS:].astype(BF16)
        fb = jnp.zeros((F_ROWS, 1), F32).at[:N_HEADS, 0].set(forget_b[l])

        qv, k, vs = _projection(x, mod[l], mod[l], wq, wk, wv, fb, cos, sin, l)

        lam = jnp.stack([lam_q1[l], lam_k1[l], lam_q2[l], lam_k2[l]])
        oa = _attention("diff", 0, qv, k, [lam, subln_g[l].reshape(HEAD_DIM, 1)],
                        [const((4, DIFF_DIM)), const((HEAD_DIM, 1))],
                        functools.partial(_diff_kernel, lambda_init=lambda_init), l, chains(2 * N_HEADS))
        ob = _attention("stick", 1, qv, k, [ut], [const((sub, sub))], _stick_kernel, l,
                        [pltpu.VMEM((N_HEADS, 1, t), F32), pltpu.VMEM((N_HEADS, HEAD_DIM, t), F32)], vs)
        oc = _attention("dilated", 2, qv, k, [bias], [const(bias.shape)], _dilated_kernel, l, chains(N_HEADS))
        od = _attention("forget", 3, qv, k, [], [], _forget_kernel, l, chains(N_HEADS))

        x = _merge(x, mod[l], (oa, ob, oc, od), wg, w_branch[l].astype(BF16), w_o[l].astype(BF16),
                   ln1_g[l], ln1_b[l], l)
        x = _conv_ffn(x, mod[l], w_up[l].astype(BF16), conv_w[l], conv_b[l], w_down[l].astype(BF16),
                      ln2_g[l], ln2_b[l], l)
    return x
```

```python
import functools
import math

import numpy as np
import jax
import jax.numpy as jnp
from jax import lax
from jax.experimental import pallas as pl
from jax.experimental.pallas import tpu as pltpu

D_MODEL = 1024
DEPTH = 2
HEAD_DIM = 64
DIFF_DIM = HEAD_DIM // 2
N_HEADS = 4
BRANCH_WIDTH = N_HEADS * HEAD_DIM
N_BRANCHES = 4
ROPE_THETA = 500000.0
ROPE_FRACTION = 4
DILATED_PATTERNS = ((128, 1), (512, 4), (2048, 16))
D_FF = 2048
CONV_WIDTH = 3
LN_EPS = 1e-5
DEEPNORM_ALPHA = (2.0 * DEPTH) ** 0.25

SEQ_TILE = 512
LANES = 128
F_ROWS = 16
ONES_ROWS = 16
NEG_BIG = -1e30
LOG2E = math.log2(math.e)
SOFTMAX_MARGIN = 64.0
STICK_EXIT = -105.0
KN_ROWS = 4 * N_HEADS
AUG = LANES
VMEM_LIMIT = 56 * 2**20

F32 = jnp.float32
BF16 = jnp.bfloat16
_NT = (((1,), (1,)), ((), ()))


def _dot(a, b):
    return jnp.dot(a, b, preferred_element_type=F32)


def _params(*sem):
    return pltpu.CompilerParams(dimension_semantics=sem, vmem_limit_bytes=VMEM_LIMIT)


def _split3(x):
    hi = x.astype(BF16)
    r1 = x - hi.astype(F32)
    mid = r1.astype(BF16)
    lo = (r1 - mid.astype(F32)).astype(BF16)
    return hi, mid, lo


def _layer_norm(r, g, b):
    mu = jnp.mean(r, axis=-1, keepdims=True)
    d = r - mu
    var = jnp.mean(d * d, axis=-1, keepdims=True)
    return d * lax.rsqrt(var + LN_EPS) * g + b


def _mod_kernel(c_ref, w_ref, b_ref, o_ref):
    c = c_ref[...]
    a = c / (1.0 + jnp.exp(-c))
    a_hi = a.astype(BF16)
    a_lo = (a - a_hi.astype(F32)).astype(BF16)
    w = w_ref[0]
    w_hi = w.astype(BF16)
    w_lo = (w - w_hi.astype(F32)).astype(BF16)
    o_ref[0] = _dot(a_hi, w_hi) + _dot(a_lo, w_hi) + _dot(a_hi, w_lo) + b_ref[0]


def _modulation(c, w_ada, b_ada):
    b = c.shape[0]
    rows = 8
    c_pad = jnp.zeros((rows, D_MODEL), F32).at[:b].set(c)
    out = pl.pallas_call(
        _mod_kernel,
        grid=(DEPTH, 6),
        in_specs=[
            pl.BlockSpec((rows, D_MODEL), lambda l, j: (0, 0)),
            pl.BlockSpec((1, D_MODEL, D_MODEL), lambda l, j: (l, 0, j)),
            pl.BlockSpec((1, 1, D_MODEL), lambda l, j: (l, 0, j)),
        ],
        out_specs=pl.BlockSpec((1, rows, D_MODEL), lambda l, j: (l, 0, j)),
        out_shape=jax.ShapeDtypeStruct((DEPTH, rows, 6 * D_MODEL), F32),
        compiler_params=_params("arbitrary", "arbitrary"),
        name="adaln_mod",
    )(c_pad, w_ada, b_ada.reshape(DEPTH, 1, 6 * D_MODEL))
    return out[:, :b].reshape(DEPTH, b, 6, 1, D_MODEL)


def _rope_kernel(pos_ref, invf_ref, sgn_ref, cos_ref, sin_ref):
    ang = invf_ref[...] * pos_ref[0].astype(F32)
    cos_ref[0] = jnp.cos(ang)
    sin_ref[0] = jnp.sin(ang) * sgn_ref[...]


def _rope_tables(positions):
    b, s = positions.shape
    r_a, r_c = DIFF_DIM // ROPE_FRACTION, HEAD_DIM // ROPE_FRACTION
    inv_a = jnp.power(jnp.float32(ROPE_THETA), -2.0 * jnp.arange(r_a // 2, dtype=F32) / r_a)
    inv_c = jnp.power(jnp.float32(ROPE_THETA), -2.0 * jnp.arange(r_c // 2, dtype=F32) / r_c)
    invf = jnp.concatenate([inv_a, inv_a, inv_c]).reshape(16, 1)
    sgn = jnp.concatenate([-jnp.ones(4, F32), jnp.ones(12, F32)]).reshape(16, 1)
    return pl.pallas_call(
        _rope_kernel,
        grid=(b,),
        in_specs=[
            pl.BlockSpec((1, 1, s), lambda i: (i, 0, 0)),
            pl.BlockSpec((16, 1), lambda i: (0, 0)),
            pl.BlockSpec((16, 1), lambda i: (0, 0)),
        ],
        out_specs=[pl.BlockSpec((1, 16, s), lambda i: (i, 0, 0))] * 2,
        out_shape=[jax.ShapeDtypeStruct((b, 16, s), F32)] * 2,
        compiler_params=_params("arbitrary"),
        name="rope_tables",
    )(positions.reshape(b, 1, s), invf, sgn)


def _rope_rows(y, cos, sin):
    cos_a, sin_a, cos_c, sin_c = cos[0:8], sin[0:8], cos[8:16], sin[8:16]
    parts = []
    for hh in range(2 * N_HEADS):
        r0 = DIFF_DIM * hh
        seg = y[r0:r0 + 8]
        parts.append(seg * cos_a + pltpu.roll(seg, 4, 0) * sin_a)
        parts.append(y[r0 + 8:r0 + DIFF_DIM])
    parts.append(y[BRANCH_WIDTH:2 * BRANCH_WIDTH])
    for hh in range(N_HEADS):
        r0 = 2 * BRANCH_WIDTH + HEAD_DIM * hh
        t0, t1 = y[r0:r0 + 8], y[r0 + 8:r0 + 16]
        parts.append(t0 * cos_c - t1 * sin_c)
        parts.append(t1 * cos_c + t0 * sin_c)
        parts.append(y[r0 + 16:r0 + HEAD_DIM])
    parts.append(y[3 * BRANCH_WIDTH:4 * BRANCH_WIDTH])
    return jnp.concatenate(parts, axis=0)


def _proj_kernel(x_ref, sc_ref, sh_ref, wq_ref, wk_ref, wv_ref, fb_ref, cos_ref, sin_ref, u_ref,
                 qv_ref, k_ref, vs_ref, kn_ref, carry_ref):
    t = x_ref.shape[1]
    nq = 4 * BRANCH_WIDTH
    n3 = 3 * BRANCH_WIDTH

    @pl.when(pl.program_id(1) == 0)
    def _():
        carry_ref[...] = jnp.zeros_like(carry_ref)

    h = (x_ref[0] * (1.0 + sc_ref[0, 0]) + sh_ref[0, 0]).astype(BF16)
    cos, sin = cos_ref[0], sin_ref[0]

    yq = _rope_rows(lax.dot_general(wq_ref[...], h, _NT, preferred_element_type=F32), cos, sin)
    yk = _rope_rows(lax.dot_general(wk_ref[...], h, _NT, preferred_element_type=F32), cos, sin)
    yv = lax.dot_general(wv_ref[...], h, _NT, preferred_element_type=F32)

    fl = yv[nq:nq + F_ROWS] + fb_ref[...]
    lf = jnp.minimum(fl, 0.0) - jnp.log(1.0 + jnp.exp(-jnp.abs(fl)))
    cs = _dot(jnp.concatenate(_split3(lf), axis=0), u_ref[...])
    fcum = cs[0:F_ROWS] + cs[F_ROWS:2 * F_ROWS] + cs[2 * F_ROWS:3 * F_ROWS] + carry_ref[:, 0:1]
    carry_ref[...] = jnp.broadcast_to(fcum[:, t - 1:t], carry_ref.shape)

    f2 = fcum * LOG2E
    row8 = lax.broadcasted_iota(jnp.int32, (8, t), 0)
    ones3 = jnp.where(row8 < 3, 1.0, 0.0)
    pad = jnp.zeros((AUG - HEAD_DIM - 16, t), F32)
    q_rows, k_rows = [], []
    for hh in range(N_HEADS):
        hi, mid, lo = (p.astype(F32) for p in _split3(f2[hh:hh + 1]))
        pieces = jnp.where(row8 == 0, hi, jnp.where(row8 == 1, mid, jnp.where(row8 == 2, lo, 0.0)))
        feat = slice(n3 + HEAD_DIM * hh, n3 + HEAD_DIM * (hh + 1))
        q_rows += [yq[feat], ones3, pieces, pad]
        k_rows += [yk[feat], -pieces, ones3, pad]
    qv_ref[0, 0, 0:N_HEADS * AUG + n3, :] = jnp.concatenate(q_rows + [yq[0:n3]], axis=0).astype(BF16)
    qv_ref[0, 0, N_HEADS * AUG + n3:N_HEADS * AUG + n3 + nq, :] = yv[0:nq].astype(BF16)
    ksq = jnp.square(yk.astype(BF16).astype(F32))
    head_rows = ([(DIFF_DIM * i, DIFF_DIM) for i in range(2 * N_HEADS)]
                 + [(2 * BRANCH_WIDTH + HEAD_DIM * i, HEAD_DIM) for i in range(N_HEADS)]
                 + [(n3 + HEAD_DIM * i, HEAD_DIM) for i in range(N_HEADS)])
    kn_ref[0] = jnp.concatenate([jnp.sum(ksq[r0:r0 + w], axis=0, keepdims=True) for r0, w in head_rows], axis=0)
    for half in range(2):
        vs_ref[0, half] = yv[BRANCH_WIDTH:2 * BRANCH_WIDTH, half * (t // 2):(half + 1) * (t // 2)].astype(BF16)
    k_ref[0] = jnp.concatenate(k_rows + [yk[0:n3]], axis=0).T.astype(BF16)


def _projection(x, sc, sh, wq, wk, wv, fb, cos, sin, l):
    b, s, d = x.shape
    t = SEQ_TILE
    ns = s // t
    nq = 4 * BRANCH_WIDTH
    nk = N_HEADS * AUG + 3 * BRANCH_WIDTH
    u = jnp.asarray(np.triu(np.ones((t, t), np.float32)), BF16)
    const = lambda shape: pl.BlockSpec(shape, lambda i, j: (0,) * len(shape))
    return pl.pallas_call(
        _proj_kernel,
        grid=(b, ns),
        in_specs=[
            pl.BlockSpec((1, t, d), lambda i, j: (i, j, 0)),
            pl.BlockSpec((1, 1, 1, d), lambda i, j: (i, 1, 0, 0)),
            pl.BlockSpec((1, 1, 1, d), lambda i, j: (i, 0, 0, 0)),
            const((nq, d)), const((nq, d)), const((nq + F_ROWS, d)),
            const((F_ROWS, 1)),
            pl.BlockSpec((1, 16, t), lambda i, j: (i, 0, j)),
            pl.BlockSpec((1, 16, t), lambda i, j: (i, 0, j)),
            const((t, t)),
        ],
        out_specs=[
            pl.BlockSpec((1, 1, nk + nq, t), lambda i, j: (i, j, 0, 0)),
            pl.BlockSpec((1, t, nk), lambda i, j: (i, j, 0)),
            pl.BlockSpec((1, 2, BRANCH_WIDTH, t // 2), lambda i, j: (i, j, 0, 0)),
            pl.BlockSpec((1, KN_ROWS, t), lambda i, j: (i, 0, j)),
        ],
        out_shape=[
            jax.ShapeDtypeStruct((b, ns, nk + nq, t), BF16),
            jax.ShapeDtypeStruct((b, s, nk), BF16),
            jax.ShapeDtypeStruct((b, 2 * ns, BRANCH_WIDTH, t // 2), BF16),
            jax.ShapeDtypeStruct((b, KN_ROWS, s), F32),
        ],
        scratch_shapes=[pltpu.VMEM((F_ROWS, LANES), F32)],
        compiler_params=_params("arbitrary", "arbitrary"),
        name=f"projection_l{l}",
    )(x, sc, sh, wq, wk, wv, fb, cos, sin, u)


def _masked_q(q_ref, head, width):
    per_group = LANES // width
    g = head // per_group
    qg = q_ref[0, 0, LANES * g:LANES * (g + 1), :]
    rows = lax.broadcasted_iota(jnp.int32, qg.shape, 0)
    r0 = width * (head % per_group)
    return jnp.where((rows >= r0) & (rows < r0 + width), qg, jnp.zeros_like(qg))


def _load_k(k_ref, kb, group, t):
    k0 = pl.multiple_of(kb * t, t)
    return k_ref[0, pl.ds(k0, t), LANES * group:LANES * (group + 1)]


def _load_v(v_ref, kb, head):
    return v_ref[0, kb, HEAD_DIM * head:HEAD_DIM * (head + 1), :]


def _softmax_step(s, carry, vblk):
    m, acc = carry
    t = s.shape[1]
    m_new = jnp.maximum(m, jnp.max(s, axis=0, keepdims=True))
    p = jnp.exp2(s - m_new).astype(BF16)
    alpha = jnp.exp2(m - m_new)
    vaug = jnp.concatenate([vblk, jnp.ones((ONES_ROWS, t), BF16)], axis=0)
    return m_new, alpha * acc + _dot(vaug, p)


def _causal_tile(t, strict):
    key = lax.broadcasted_iota(jnp.int32, (t, t), 0)
    qry = lax.broadcasted_iota(jnp.int32, (t, t), 1)
    return key < qry if strict else key <= qry


def _staggered(n, score_fn, step_fn, first=None, before_last=None):
    out = []
    s_next = score_fn(0) if first is None else first
    for h in range(n):
        s = s_next
        if h + 1 < n:
            s_next = score_fn(h + 1)
        elif before_last is not None:
            before_last()
        out.append(step_fn(h, s))
    return tuple(out)


def _softmax_blocks(n, score, adjust, vblock, scratch, lo, qb, t, bounds):
    s0_ref, m_ref, acc_ref = scratch
    ones = jnp.ones((ONES_ROWS, t), BF16)

    def prefetch_before(kb):
        def prefetch():
            s0_ref[...] = score(jnp.maximum(kb - 1, 0), 0)
        return prefetch

    def first_step(h, s):
        s = adjust(qb, h, s, True)
        m = jnp.max(s, axis=0, keepdims=True)
        m_ref[h] = m
        acc_ref[h] = _dot(jnp.concatenate([vblock(qb, h), ones], axis=0), jnp.exp2(s - m).astype(BF16))

    _staggered(n, lambda h: score(qb, h), first_step, before_last=prefetch_before(qb))

    def loop_body(i, _):
        kb = qb - 1 - i
        excess = []

        def fast_step(h, s):
            m = m_ref[h]
            s = adjust(kb, h, s, False)
            excess.append(jnp.max(s, axis=0, keepdims=True) - m)
            return _dot(jnp.concatenate([vblock(kb, h), ones], axis=0), jnp.exp2(s - m).astype(BF16))

        deltas = _staggered(n, lambda h: score(kb, h), fast_step, first=s0_ref[...],
                            before_last=prefetch_before(kb))
        over = jnp.max(functools.reduce(jnp.maximum, excess)) > SOFTMAX_MARGIN

        @pl.when(jnp.logical_not(over))
        def _():
            for h in range(n):
                acc_ref[h] += deltas[h]

        @pl.when(over)
        def _():
            def exact_step(h, s):
                m_ref[h], acc_ref[h] = _softmax_step(adjust(kb, h, s, False), (m_ref[h], acc_ref[h]), vblock(kb, h))

            _staggered(n, lambda h: score(kb, h), exact_step)

        return 0

    def unchecked_body(i, _):
        kb = qb - 1 - i

        def step(h, s):
            p = jnp.exp2(adjust(kb, h, s, False) - m_ref[h]).astype(BF16)
            acc_ref[h] += _dot(jnp.concatenate([vblock(kb, h), ones], axis=0), p)

        _staggered(n, lambda h: score(kb, h), step, first=s0_ref[...], before_last=prefetch_before(kb))
        return 0

    safe = jnp.max(functools.reduce(jnp.maximum, [bounds[h] - m_ref[h] for h in range(n)])) <= SOFTMAX_MARGIN

    @pl.when(safe)
    def _():
        lax.fori_loop(0, qb - lo, unchecked_body, 0)

    @pl.when(jnp.logical_not(safe))
    def _():
        lax.fori_loop(0, qb - lo, loop_body, 0)

    return [acc_ref[h, 0:HEAD_DIM] / acc_ref[h, HEAD_DIM:HEAD_DIM + 1] for h in range(n)]


def _score_bounds(q_heads, kn_ref, row0, extra):
    out = []
    for h, q in enumerate(q_heads):
        qn2 = jnp.sum(jnp.square(q.astype(F32)), axis=0, keepdims=True)
        kn2 = jnp.max(kn_ref[0, row0 + h:row0 + h + 1, :], axis=1, keepdims=True)
        out.append(jnp.sqrt(qn2 * kn2) * 1.001 + (1.0 + extra))
    return out


def _diff_kernel(q_ref, k_ref, v_ref, kn_ref, lam_ref, g_ref, o_ref, *scratch, lambda_init):
    qb = pl.program_id(1)
    t = q_ref.shape[-1]
    n = 2 * N_HEADS
    per_group = LANES // DIFF_DIM
    lam_p = lam_ref[...]
    lam = (jnp.exp(jnp.sum(lam_p[0:1] * lam_p[1:2], keepdims=True))
           - jnp.exp(jnp.sum(lam_p[2:3] * lam_p[3:4], keepdims=True)) + lambda_init)
    qms = [_masked_q(q_ref, head, DIFF_DIM) for head in range(n)]

    def score(kb, head):
        return _dot(_load_k(k_ref, kb, head // per_group, t), qms[head])

    def adjust(kb, head, s, diag):
        return jnp.where(_causal_tile(t, False), s, NEG_BIG) if diag else s

    comps = _softmax_blocks(n, score, adjust, lambda kb, head: _load_v(v_ref, kb, head // 2), scratch, 0, qb, t,
                            _score_bounds(qms, kn_ref, 0, 0.0))
    outs = []
    for hh in range(N_HEADS):
        o = comps[2 * hh] - lam * comps[2 * hh + 1]
        o = o * lax.rsqrt(jnp.mean(o * o, axis=0, keepdims=True) + LN_EPS)
        outs.append(o * (g_ref[...] * (1.0 - lambda_init)))
    o_ref[0] = jnp.concatenate(outs, axis=0).T.astype(BF16)


def _stick_kernel(q_ref, k_ref, v_ref, ut_ref, o_ref, run_ref, acc_ref):
    qb = pl.program_id(1)
    t = q_ref.shape[-1]
    sub = t // 2
    per_group = LANES // HEAD_DIM
    qms = [_masked_q(q_ref, head, HEAD_DIM) for head in range(N_HEADS)]
    run_ref[...] = jnp.zeros_like(run_ref)
    acc_ref[...] = jnp.zeros_like(acc_ref)

    def half_step(hb, q0, key_off):
        cols = slice(q0, t)
        k0 = pl.multiple_of(hb * sub, sub)
        kblks = [k_ref[0, pl.ds(k0, sub), LANES * g:LANES * (g + 1)] for g in range(N_HEADS // per_group)]
        if key_off is not None:
            key = lax.broadcasted_iota(jnp.int32, (sub, t - q0), 0) + key_off
            qry = lax.broadcasted_iota(jnp.int32, (sub, t - q0), 1) + q0
            valid = key < qry

        def step(head, z):
            lk = -(jnp.maximum(z, 0.0) + jnp.log(1.0 + jnp.exp(-jnp.abs(z))))
            lkm = lk if key_off is None else jnp.where(valid, lk, 0.0)
            hi = lkm.astype(BF16)
            lo = (lkm - hi.astype(F32)).astype(BF16)
            within = _dot(ut_ref[...], hi) + _dot(ut_ref[...], lo)
            a = jnp.exp(z + lk + within + run_ref[head, :, cols])
            if key_off is not None:
                a = jnp.where(valid, a, 0.0)
            acc_ref[head, :, cols] += _dot(v_ref[0, hb, HEAD_DIM * head:HEAD_DIM * (head + 1), :], a.astype(BF16))
            run_ref[head, :, cols] += within[0:1] + lkm[0:1]

        _staggered(N_HEADS, lambda head: _dot(kblks[head // per_group], qms[head][:, cols]), step)

    def alive():
        worst = functools.reduce(jnp.maximum, [run_ref[head] for head in range(N_HEADS)])
        return jnp.max(worst) > STICK_EXIT

    half_step(2 * qb + 1, sub, sub)
    half_step(2 * qb, 0, 0)

    def earlier(state):
        i, _ = state
        half_step(2 * qb - 1 - i, 0, None)
        return i + 1, alive()

    lax.while_loop(lambda st: jnp.logical_and(st[0] < 2 * qb, st[1]), earlier, (jnp.int32(0), alive()))
    o_ref[0] = jnp.concatenate([acc_ref[head] for head in range(N_HEADS)], axis=0).T.astype(BF16)


def _dilated_kernel(q_ref, k_ref, v_ref, kn_ref, bias_ref, o_ref, *scratch):
    qb = pl.program_id(1)
    t = q_ref.shape[-1]
    n_back = bias_ref.shape[0] - 1
    per_group = LANES // HEAD_DIM
    qms = [_masked_q(q_ref, head, HEAD_DIM) for head in range(N_HEADS)]

    def score(kb, head):
        return _dot(_load_k(k_ref, kb, head // per_group, t), qms[head])

    def adjust(kb, head, s, diag):
        return bias_ref[qb - kb] + s

    outs = _softmax_blocks(N_HEADS, score, adjust, lambda kb, head: _load_v(v_ref, kb, head), scratch,
                           jnp.maximum(qb - n_back, 0), qb, t,
                           _score_bounds(qms, kn_ref, 2 * N_HEADS, math.log2(len(DILATED_PATTERNS))))
    o_ref[0] = jnp.concatenate(outs, axis=0).T.astype(BF16)


def _forget_kernel(q_ref, k_ref, v_ref, kn_ref, o_ref, *scratch):
    qb = pl.program_id(1)
    t = q_ref.shape[-1]
    qas = [q_ref[0, 0, AUG * head:AUG * (head + 1), :] for head in range(N_HEADS)]

    def score(kb, head):
        return _dot(_load_k(k_ref, kb, head, t), qas[head])

    def adjust(kb, head, s, diag):
        return jnp.where(_causal_tile(t, False), s, NEG_BIG) if diag else s

    bounds = _score_bounds([qa[0:HEAD_DIM] for qa in qas], kn_ref, 3 * N_HEADS, 0.0)
    outs = _softmax_blocks(N_HEADS, score, adjust, lambda kb, head: _load_v(v_ref, kb, head), scratch, 0, qb, t,
                           bounds)
    o_ref[0] = jnp.concatenate(outs, axis=0).T.astype(BF16)


def _dilated_bias(t):
    n_back = max(w for w, _ in DILATED_PATTERNS) // t
    kk = np.arange(t)[:, None]
    qq = np.arange(t)[None, :]
    tiles = []
    for d in range(n_back + 1):
        dist = d * t + qq - kk
        count = np.zeros((t, t), np.float64)
        for window, dil in DILATED_PATTERNS:
            count += (dist >= 0) & (dist % dil == 0) & (dist // dil <= window // dil)
        tiles.append(np.where(count > 0, np.log2(np.maximum(count, 1.0)), NEG_BIG))
    return jnp.asarray(np.stack(tiles), F32)


def _attention(kind, branch, qv, k, extra_inputs, extra_specs, kernel, l, scratch=(), v_half=None):
    b, ns, _, t = qv.shape
    s = ns * t
    n_aug = N_HEADS * AUG
    if branch == 3:
        width, qk_block = n_aug, 0
    else:
        width, qk_block = BRANCH_WIDTH, n_aug // BRANCH_WIDTH + branch
    v_block = (n_aug + 3 * BRANCH_WIDTH) // BRANCH_WIDTH + branch
    in_specs = [
        pl.BlockSpec((1, 1, width, t), lambda i, j: (i, j, qk_block, 0)),
        pl.BlockSpec((1, s, width), lambda i, j: (i, 0, qk_block)),
        pl.BlockSpec((1, ns, BRANCH_WIDTH, t), lambda i, j: (i, 0, v_block, 0)) if v_half is None else
        pl.BlockSpec((1, 2 * ns, BRANCH_WIDTH, t // 2), lambda i, j: (i, 0, 0, 0)),
    ] + extra_specs
    return pl.pallas_call(
        kernel,
        grid=(b, ns),
        in_specs=in_specs,
        out_specs=pl.BlockSpec((1, t, BRANCH_WIDTH), lambda i, j: (i, j, 0)),
        out_shape=jax.ShapeDtypeStruct((b, s, BRANCH_WIDTH), BF16),
        scratch_shapes=list(scratch),
        compiler_params=_params("arbitrary", "arbitrary"),
        name=f"{kind}_attention_l{l}",
    )(qv, k, qv if v_half is None else v_half, *extra_inputs)


def _merge_kernel(x_ref, sc_ref, sh_ref, ga_ref, oa_ref, ob_ref, oc_ref, od_ref, wg_ref, wb_ref, wo_ref,
                  lng_ref, lnb_ref, out_ref):
    x = x_ref[0]
    d = x.shape[-1]
    h = (x * (1.0 + sc_ref[0, 0]) + sh_ref[0, 0]).astype(BF16)
    merged = None
    for n, o_ref in enumerate((oa_ref, ob_ref, oc_ref, od_ref)):
        gate = 1.0 / (1.0 + jnp.exp(-_dot(h, wg_ref[:, d * n:d * (n + 1)])))
        y = gate * _dot(o_ref[0], wb_ref[n])
        merged = y if merged is None else merged + y
    mix = _dot(merged.astype(BF16), wo_ref[...])
    out_ref[0] = _layer_norm(DEEPNORM_ALPHA * x + ga_ref[0, 0] * mix, lng_ref[...], lnb_ref[...])


def _merge(x, mod, outs, wg, wb, wo, ln_g, ln_b, l):
    b, s, d = x.shape
    t = SEQ_TILE
    const = lambda shape: pl.BlockSpec(shape, lambda i, j: (0,) * len(shape))
    mod_spec = lambda idx: pl.BlockSpec((1, 1, 1, d), lambda i, j: (i, idx, 0, 0))
    o_spec = pl.BlockSpec((1, t, BRANCH_WIDTH), lambda i, j: (i, j, 0))
    return pl.pallas_call(
        _merge_kernel,
        grid=(b, s // t),
        in_specs=[pl.BlockSpec((1, t, d), lambda i, j: (i, j, 0)), mod_spec(1), mod_spec(0), mod_spec(2),
                  o_spec, o_spec, o_spec, o_spec,
                  const((d, N_BRANCHES * d)), const((N_BRANCHES, BRANCH_WIDTH, d)), const((d, d)),
                  const((1, d)), const((1, d))],
        out_specs=pl.BlockSpec((1, t, d), lambda i, j: (i, j, 0)),
        out_shape=jax.ShapeDtypeStruct((b, s, d), F32),
        compiler_params=_params("arbitrary", "arbitrary"),
        name=f"merge_l{l}",
    )(x, mod, mod, mod, *outs, wg, wb, wo, ln_g.reshape(1, d), ln_b.reshape(1, d))


def _ffn_kernel(x_ref, sc_ref, sh_ref, gf_ref, wup_ref, cw_ref, cb_ref, wdn_ref, lng_ref, lnb_ref,
                out_ref, tail_ref):
    t = x_ref.shape[1]
    chunk = D_FF // 2

    @pl.when(pl.program_id(1) == 0)
    def _():
        tail_ref[...] = jnp.zeros_like(tail_ref)

    x = x_ref[0]
    h = (x * (1.0 + sc_ref[0, 0]) + sh_ref[0, 0]).astype(BF16)
    rows = lax.broadcasted_iota(jnp.int32, (8, chunk), 0)

    def conv(col0):
        cols = slice(col0, col0 + chunk)
        u = _dot(h, wup_ref[:, cols])
        tail = tail_ref[:, cols]
        u1, u2 = pltpu.roll(u, 1, 0), pltpu.roll(u, 2, 0)
        head1 = jnp.where(rows < 1, pltpu.roll(tail, 1, 0), u1[0:8])
        head2 = jnp.where(rows < 2, pltpu.roll(tail, 2, 0), u2[0:8])
        u1 = jnp.concatenate([head1, u1[8:]], axis=0)
        u2 = jnp.concatenate([head2, u2[8:]], axis=0)
        tail_ref[:, cols] = u[t - 8:t]
        return cw_ref[2:3, cols] * u + cw_ref[1:2, cols] * u1 + cw_ref[0:1, cols] * u2 + cb_ref[:, cols]

    ffn = None
    for c in range(D_FF // chunk):
        a = conv(c * chunk)
        g = conv(D_FF + c * chunk)
        act = (a / (1.0 + jnp.exp(-a)) * g).astype(BF16)
        y = _dot(act, wdn_ref[c * chunk:(c + 1) * chunk, :])
        ffn = y if ffn is None else ffn + y
    out_ref[0] = _layer_norm(DEEPNORM_ALPHA * x + gf_ref[0, 0] * ffn, lng_ref[...], lnb_ref[...])


def _conv_ffn(x, mod, wup, cw, cb, wdn, ln_g, ln_b, l):
    b, s, d = x.shape
    t = SEQ_TILE
    const = lambda shape: pl.BlockSpec(shape, lambda i, j: (0,) * len(shape))
    mod_spec = lambda idx: pl.BlockSpec((1, 1, 1, d), lambda i, j: (i, idx, 0, 0))
    return pl.pallas_call(
        _ffn_kernel,
        grid=(b, s // t),
        in_specs=[pl.BlockSpec((1, t, d), lambda i, j: (i, j, 0)), mod_spec(4), mod_spec(3), mod_spec(5),
                  const((d, 2 * D_FF)), const((CONV_WIDTH, 2 * D_FF)), const((1, 2 * D_FF)), const((D_FF, d)),
                  const((1, d)), const((1, d))],
        out_specs=pl.BlockSpec((1, t, d), lambda i, j: (i, j, 0)),
        out_shape=jax.ShapeDtypeStruct((b, s, d), F32),
        scratch_shapes=[pltpu.VMEM((8, 2 * D_FF), F32)],
        compiler_params=_params("arbitrary", "arbitrary"),
        name=f"conv_ffn_l{l}",
    )(x, mod, mod, mod, wup, cw, cb.reshape(1, 2 * D_FF), wdn, ln_g.reshape(1, d), ln_b.reshape(1, d))


def kernel(x, c, positions, w_ada, b_ada, w_in, lam_q1, lam_k1, lam_q2, lam_k2, subln_g, forget_b, w_branch,
           w_o, ln1_g, ln1_b, w_up, conv_w, conv_b, w_down, ln2_g, ln2_b):
    b, s, d = x.shape
    t = SEQ_TILE
    bw = BRANCH_WIDTH
    const = lambda shape: pl.BlockSpec(shape, lambda i, j: (0,) * len(shape))

    mod = _modulation(c, w_ada, b_ada)
    cos, sin = _rope_tables(positions)
    bias = _dilated_bias(t)
    sub = t // 2
    ut = jnp.asarray(np.triu(np.ones((sub, sub), np.float32), 1), BF16)
    chains = lambda n: [pltpu.VMEM((t, t), F32), pltpu.VMEM((n, 1, t), F32),
                        pltpu.VMEM((n, HEAD_DIM + ONES_ROWS, t), F32)]

    for l in range(DEPTH):
        lambda_init = 0.8 - 0.6 * math.exp(-0.3 * l)
        w = w_in[l]
        col = lambda i: w[:, bw * i:bw * (i + 1)]
        wq = jnp.concatenate([col(0) * (DIFF_DIM ** -0.5 * LOG2E), col(3) * HEAD_DIM ** -0.5,
                              col(6) * (HEAD_DIM ** -0.5 * LOG2E), col(9) * (HEAD_DIM ** -0.5 * LOG2E)],
                             axis=1).T.astype(BF16)
        wk = jnp.concatenate([col(1), col(4), col(7), col(10)], axis=1).T.astype(BF16)
        wf = jnp.zeros((d, F_ROWS), F32).at[:, :N_HEADS].set(w[:, 12 * bw:12 * bw + N_HEADS])
        wv = jnp.concatenate([col(2), col(5), col(8), col(11), wf], axis=1).T.astype(BF16)
        wg = w[:, 12 * bw + N_HEADS:].astype(BF16)
        fb = jnp.zeros((F_ROWS, 1), F32).at[:N_HEADS, 0].set(forget_b[l])

        qv, k, vs, kn = _projection(x, mod[l], mod[l], wq, wk, wv, fb, cos, sin, l)

        lam = jnp.stack([lam_q1[l], lam_k1[l], lam_q2[l], lam_k2[l]])
        kn_spec = pl.BlockSpec((1, KN_ROWS, s), lambda i, j: (i, 0, 0))
        oa = _attention("diff", 0, qv, k, [kn, lam, subln_g[l].reshape(HEAD_DIM, 1)],
                        [kn_spec, const((4, DIFF_DIM)), const((HEAD_DIM, 1))],
                        functools.partial(_diff_kernel, lambda_init=lambda_init), l, chains(2 * N_HEADS))
        ob = _attention("stick", 1, qv, k, [ut], [const((sub, sub))], _stick_kernel, l,
                        [pltpu.VMEM((N_HEADS, 1, t), F32), pltpu.VMEM((N_HEADS, HEAD_DIM, t), F32)], vs)
        oc = _attention("dilated", 2, qv, k, [kn, bias], [kn_spec, const(bias.shape)], _dilated_kernel, l,
                        chains(N_HEADS))
        od = _attention("forget", 3, qv, k, [kn], [kn_spec], _forget_kernel, l, chains(N_HEADS))

        x = _merge(x, mod[l], (oa, ob, oc, od), wg, w_branch[l].astype(BF16), w_o[l].astype(BF16),
                   ln1_g[l], ln1_b[l], l)
        x = _conv_ffn(x, mod[l], w_up[l].astype(BF16), conv_w[l], conv_b[l], w_down[l].astype(BF16),
                      ln2_g[l], ln2_b[l], l)
    return x
```

```python
import functools
import math

import numpy as np
import jax
import jax.numpy as jnp
from jax import lax
from jax.experimental import pallas as pl
from jax.experimental.pallas import tpu as pltpu

D_MODEL = 1024
DEPTH = 2
HEAD_DIM = 64
DIFF_DIM = HEAD_DIM // 2
N_HEADS = 4
BRANCH_WIDTH = N_HEADS * HEAD_DIM
N_BRANCHES = 4
ROPE_THETA = 500000.0
ROPE_FRACTION = 4
DILATED_PATTERNS = ((128, 1), (512, 4), (2048, 16))
D_FF = 2048
CONV_WIDTH = 3
LN_EPS = 1e-5
DEEPNORM_ALPHA = (2.0 * DEPTH) ** 0.25

SEQ_TILE = 512
LANES = 128
F_ROWS = 16
ONES_ROWS = 16
NEG_BIG = -1e30
LOG2E = math.log2(math.e)
SOFTMAX_MARGIN = 64.0
FUSED_STICK_STEPS = 3
STICK_EXIT = -105.0
KN_ROWS = 4 * N_HEADS
AUG = LANES
VMEM_LIMIT = 56 * 2**20

F32 = jnp.float32
BF16 = jnp.bfloat16
_NT = (((1,), (1,)), ((), ()))


def _dot(a, b):
    return jnp.dot(a, b, preferred_element_type=F32)


def _params(*sem):
    return pltpu.CompilerParams(dimension_semantics=sem, vmem_limit_bytes=VMEM_LIMIT)


def _split3(x):
    hi = x.astype(BF16)
    r1 = x - hi.astype(F32)
    mid = r1.astype(BF16)
    lo = (r1 - mid.astype(F32)).astype(BF16)
    return hi, mid, lo


def _layer_norm(r, g, b):
    mu = jnp.mean(r, axis=-1, keepdims=True)
    d = r - mu
    var = jnp.mean(d * d, axis=-1, keepdims=True)
    return d * lax.rsqrt(var + LN_EPS) * g + b


def _mod_kernel(c_ref, w_ref, b_ref, o_ref):
    c = c_ref[...]
    a = c / (1.0 + jnp.exp(-c))
    a_hi = a.astype(BF16)
    a_lo = (a - a_hi.astype(F32)).astype(BF16)
    w = w_ref[0]
    w_hi = w.astype(BF16)
    w_lo = (w - w_hi.astype(F32)).astype(BF16)
    o_ref[0] = _dot(a_hi, w_hi) + _dot(a_lo, w_hi) + _dot(a_hi, w_lo) + b_ref[0]


def _modulation(c, w_ada, b_ada):
    b = c.shape[0]
    rows = 8
    c_pad = jnp.zeros((rows, D_MODEL), F32).at[:b].set(c)
    out = pl.pallas_call(
        _mod_kernel,
        grid=(DEPTH, 6),
        in_specs=[
            pl.BlockSpec((rows, D_MODEL), lambda l, j: (0, 0)),
            pl.BlockSpec((1, D_MODEL, D_MODEL), lambda l, j: (l, 0, j)),
            pl.BlockSpec((1, 1, D_MODEL), lambda l, j: (l, 0, j)),
        ],
        out_specs=pl.BlockSpec((1, rows, D_MODEL), lambda l, j: (l, 0, j)),
        out_shape=jax.ShapeDtypeStruct((DEPTH, rows, 6 * D_MODEL), F32),
        compiler_params=_params("arbitrary", "arbitrary"),
        name="adaln_mod",
    )(c_pad, w_ada, b_ada.reshape(DEPTH, 1, 6 * D_MODEL))
    return out[:, :b].reshape(DEPTH, b, 6, 1, D_MODEL)


def _rope_kernel(pos_ref, invf_ref, sgn_ref, cos_ref, sin_ref):
    ang = invf_ref[...] * pos_ref[0].astype(F32)
    cos_ref[0] = jnp.cos(ang)
    sin_ref[0] = jnp.sin(ang) * sgn_ref[...]


def _rope_tables(positions):
    b, s = positions.shape
    r_a, r_c = DIFF_DIM // ROPE_FRACTION, HEAD_DIM // ROPE_FRACTION
    inv_a = jnp.power(jnp.float32(ROPE_THETA), -2.0 * jnp.arange(r_a // 2, dtype=F32) / r_a)
    inv_c = jnp.power(jnp.float32(ROPE_THETA), -2.0 * jnp.arange(r_c // 2, dtype=F32) / r_c)
    invf = jnp.concatenate([inv_a, inv_a, inv_c]).reshape(16, 1)
    sgn = jnp.concatenate([-jnp.ones(4, F32), jnp.ones(12, F32)]).reshape(16, 1)
    return pl.pallas_call(
        _rope_kernel,
        grid=(b,),
        in_specs=[
            pl.BlockSpec((1, 1, s), lambda i: (i, 0, 0)),
            pl.BlockSpec((16, 1), lambda i: (0, 0)),
            pl.BlockSpec((16, 1), lambda i: (0, 0)),
        ],
        out_specs=[pl.BlockSpec((1, 16, s), lambda i: (i, 0, 0))] * 2,
        out_shape=[jax.ShapeDtypeStruct((b, 16, s), F32)] * 2,
        compiler_params=_params("arbitrary"),
        name="rope_tables",
    )(positions.reshape(b, 1, s), invf, sgn)


def _rope_rows(y, cos, sin):
    cos_a, sin_a, cos_c, sin_c = cos[0:8], sin[0:8], cos[8:16], sin[8:16]
    parts = []
    for hh in range(2 * N_HEADS):
        r0 = DIFF_DIM * hh
        seg = y[r0:r0 + 8]
        parts.append(seg * cos_a + pltpu.roll(seg, 4, 0) * sin_a)
        parts.append(y[r0 + 8:r0 + DIFF_DIM])
    parts.append(y[BRANCH_WIDTH:2 * BRANCH_WIDTH])
    for hh in range(N_HEADS):
        r0 = 2 * BRANCH_WIDTH + HEAD_DIM * hh
        t0, t1 = y[r0:r0 + 8], y[r0 + 8:r0 + 16]
        parts.append(t0 * cos_c - t1 * sin_c)
        parts.append(t1 * cos_c + t0 * sin_c)
        parts.append(y[r0 + 16:r0 + HEAD_DIM])
    parts.append(y[3 * BRANCH_WIDTH:4 * BRANCH_WIDTH])
    return jnp.concatenate(parts, axis=0)


def _proj_kernel(x_ref, sc_ref, sh_ref, wq_ref, wk_ref, wv_ref, fb_ref, cos_ref, sin_ref, u_ref,
                 qv_ref, k_ref, vs_ref, kn_ref, carry_ref):
    t = x_ref.shape[1]
    nq = 4 * BRANCH_WIDTH
    n3 = 3 * BRANCH_WIDTH

    @pl.when(pl.program_id(1) == 0)
    def _():
        carry_ref[...] = jnp.zeros_like(carry_ref)

    h = (x_ref[0] * (1.0 + sc_ref[0, 0]) + sh_ref[0, 0]).astype(BF16)
    cos, sin = cos_ref[0], sin_ref[0]

    yq = _rope_rows(lax.dot_general(wq_ref[...], h, _NT, preferred_element_type=F32), cos, sin)
    yk = _rope_rows(lax.dot_general(wk_ref[...], h, _NT, preferred_element_type=F32), cos, sin)
    yv = lax.dot_general(wv_ref[...], h, _NT, preferred_element_type=F32)

    fl = yv[nq:nq + F_ROWS] + fb_ref[...]
    lf = jnp.minimum(fl, 0.0) - jnp.log(1.0 + jnp.exp(-jnp.abs(fl)))
    cs = _dot(jnp.concatenate(_split3(lf), axis=0), u_ref[...])
    fcum = cs[0:F_ROWS] + cs[F_ROWS:2 * F_ROWS] + cs[2 * F_ROWS:3 * F_ROWS] + carry_ref[:, 0:1]
    carry_ref[...] = jnp.broadcast_to(fcum[:, t - 1:t], carry_ref.shape)

    f2 = fcum * LOG2E
    row8 = lax.broadcasted_iota(jnp.int32, (8, t), 0)
    ones3 = jnp.where(row8 < 3, 1.0, 0.0)
    pad = jnp.zeros((AUG - HEAD_DIM - 16, t), F32)
    q_rows, k_rows = [], []
    for hh in range(N_HEADS):
        hi, mid, lo = (p.astype(F32) for p in _split3(f2[hh:hh + 1]))
        pieces = jnp.where(row8 == 0, hi, jnp.where(row8 == 1, mid, jnp.where(row8 == 2, lo, 0.0)))
        feat = slice(n3 + HEAD_DIM * hh, n3 + HEAD_DIM * (hh + 1))
        q_rows += [yq[feat], ones3, pieces, pad]
        k_rows += [yk[feat], -pieces, ones3, pad]
    qv_ref[0, 0, 0:N_HEADS * AUG + n3, :] = jnp.concatenate(q_rows + [yq[0:n3]], axis=0).astype(BF16)
    qv_ref[0, 0, N_HEADS * AUG + n3:N_HEADS * AUG + n3 + nq, :] = yv[0:nq].astype(BF16)
    ksq = jnp.square(yk.astype(BF16).astype(F32))
    head_rows = ([(DIFF_DIM * i, DIFF_DIM) for i in range(2 * N_HEADS)]
                 + [(2 * BRANCH_WIDTH + HEAD_DIM * i, HEAD_DIM) for i in range(N_HEADS)]
                 + [(n3 + HEAD_DIM * i, HEAD_DIM) for i in range(N_HEADS)])
    kn_ref[0] = jnp.concatenate([jnp.sum(ksq[r0:r0 + w], axis=0, keepdims=True) for r0, w in head_rows], axis=0)
    for half in range(2):
        vs_ref[0, half] = yv[BRANCH_WIDTH:2 * BRANCH_WIDTH, half * (t // 2):(half + 1) * (t // 2)].astype(BF16)
    k_ref[0] = jnp.concatenate(k_rows + [yk[0:n3]], axis=0).T.astype(BF16)


def _projection(x, sc, sh, wq, wk, wv, fb, cos, sin, l):
    b, s, d = x.shape
    t = SEQ_TILE
    ns = s // t
    nq = 4 * BRANCH_WIDTH
    nk = N_HEADS * AUG + 3 * BRANCH_WIDTH
    u = jnp.asarray(np.triu(np.ones((t, t), np.float32)), BF16)
    const = lambda shape: pl.BlockSpec(shape, lambda i, j: (0,) * len(shape))
    return pl.pallas_call(
        _proj_kernel,
        grid=(b, ns),
        in_specs=[
            pl.BlockSpec((1, t, d), lambda i, j: (i, j, 0)),
            pl.BlockSpec((1, 1, 1, d), lambda i, j: (i, 1, 0, 0)),
            pl.BlockSpec((1, 1, 1, d), lambda i, j: (i, 0, 0, 0)),
            const((nq, d)), const((nq, d)), const((nq + F_ROWS, d)),
            const((F_ROWS, 1)),
            pl.BlockSpec((1, 16, t), lambda i, j: (i, 0, j)),
            pl.BlockSpec((1, 16, t), lambda i, j: (i, 0, j)),
            const((t, t)),
        ],
        out_specs=[
            pl.BlockSpec((1, 1, nk + nq, t), lambda i, j: (i, j, 0, 0)),
            pl.BlockSpec((1, t, nk), lambda i, j: (i, j, 0)),
            pl.BlockSpec((1, 2, BRANCH_WIDTH, t // 2), lambda i, j: (i, j, 0, 0)),
            pl.BlockSpec((1, KN_ROWS, t), lambda i, j: (i, 0, j)),
        ],
        out_shape=[
            jax.ShapeDtypeStruct((b, ns, nk + nq, t), BF16),
            jax.ShapeDtypeStruct((b, s, nk), BF16),
            jax.ShapeDtypeStruct((b, 2 * ns, BRANCH_WIDTH, t // 2), BF16),
            jax.ShapeDtypeStruct((b, KN_ROWS, s), F32),
        ],
        scratch_shapes=[pltpu.VMEM((F_ROWS, LANES), F32)],
        compiler_params=_params("arbitrary", "arbitrary"),
        name=f"projection_l{l}",
    )(x, sc, sh, wq, wk, wv, fb, cos, sin, u)


def _masked_q(q_ref, head, width):
    per_group = LANES // width
    g = head // per_group
    qg = q_ref[0, 0, LANES * g:LANES * (g + 1), :]
    rows = lax.broadcasted_iota(jnp.int32, qg.shape, 0)
    r0 = width * (head % per_group)
    return jnp.where((rows >= r0) & (rows < r0 + width), qg, jnp.zeros_like(qg))


def _load_k(k_ref, kb, group, t):
    k0 = pl.multiple_of(kb * t, t)
    return k_ref[0, pl.ds(k0, t), LANES * group:LANES * (group + 1)]


def _load_v(v_ref, kb, head):
    return v_ref[0, kb, HEAD_DIM * head:HEAD_DIM * (head + 1), :]


def _softmax_step(s, carry, vblk):
    m, acc = carry
    t = s.shape[1]
    m_new = jnp.maximum(m, jnp.max(s, axis=0, keepdims=True))
    p = jnp.exp2(s - m_new).astype(BF16)
    alpha = jnp.exp2(m - m_new)
    vaug = jnp.concatenate([vblk, jnp.ones((ONES_ROWS, t), BF16)], axis=0)
    return m_new, alpha * acc + _dot(vaug, p)


def _causal_tile(t, strict):
    key = lax.broadcasted_iota(jnp.int32, (t, t), 0)
    qry = lax.broadcasted_iota(jnp.int32, (t, t), 1)
    return key < qry if strict else key <= qry


def _staggered(n, score_fn, step_fn, first=None, before_last=None):
    out = []
    s_next = score_fn(0) if first is None else first
    for h in range(n):
        s = s_next
        if h + 1 < n:
            s_next = score_fn(h + 1)
        elif before_last is not None:
            before_last()
        out.append(step_fn(h, s))
    return tuple(out)


def _run_chains(chains, first=None, before_last=None):
    _staggered(len(chains), lambda c: chains[c][0](), lambda c, s: chains[c][1](s), first, before_last)


def _interleave(major, minor):
    out, placed = [], 0
    for i, chain in enumerate(major):
        out.append(chain)
        upto = (i + 1) * len(minor) // len(major)
        out += minor[placed:upto]
        placed = upto
    return out


def _checked_blocks(n, score, adjust, vblock, scratch, lo, qb, t):
    s0_ref, m_ref, acc_ref = scratch
    ones = jnp.ones((ONES_ROWS, t), BF16)

    def prefetch_before(kb):
        def prefetch():
            s0_ref[...] = score(jnp.maximum(kb - 1, 0), 0)
        return prefetch

    def loop_body(i, _):
        kb = qb - 1 - i
        excess = []

        def fast_step(h, s):
            m = m_ref[h]
            s = adjust(kb, h, s, False)
            excess.append(jnp.max(s, axis=0, keepdims=True) - m)
            return _dot(jnp.concatenate([vblock(kb, h), ones], axis=0), jnp.exp2(s - m).astype(BF16))

        deltas = _staggered(n, lambda h: score(kb, h), fast_step, first=s0_ref[...],
                            before_last=prefetch_before(kb))
        over = jnp.max(functools.reduce(jnp.maximum, excess)) > SOFTMAX_MARGIN

        @pl.when(jnp.logical_not(over))
        def _():
            for h in range(n):
                acc_ref[h] += deltas[h]

        @pl.when(over)
        def _():
            def exact_step(h, s):
                m_ref[h], acc_ref[h] = _softmax_step(adjust(kb, h, s, False), (m_ref[h], acc_ref[h]), vblock(kb, h))

            _staggered(n, lambda h: score(kb, h), exact_step)

        return 0

    lax.fori_loop(0, qb - lo, loop_body, 0)


def _softmax_blocks(n, score, adjust, vblock, scratch, lo, qb, t, bounds):
    s0_ref, m_ref, acc_ref = scratch
    ones = jnp.ones((ONES_ROWS, t), BF16)

    def prefetch_before(kb):
        def prefetch():
            s0_ref[...] = score(jnp.maximum(kb - 1, 0), 0)
        return prefetch

    def first_step(h, s):
        s = adjust(qb, h, s, True)
        m = jnp.max(s, axis=0, keepdims=True)
        m_ref[h] = m
        acc_ref[h] = _dot(jnp.concatenate([vblock(qb, h), ones], axis=0), jnp.exp2(s - m).astype(BF16))

    _staggered(n, lambda h: score(qb, h), first_step, before_last=prefetch_before(qb))

    def unchecked_body(i, _):
        kb = qb - 1 - i

        def step(h, s):
            p = jnp.exp2(adjust(kb, h, s, False) - m_ref[h]).astype(BF16)
            acc_ref[h] += _dot(jnp.concatenate([vblock(kb, h), ones], axis=0), p)

        _staggered(n, lambda h: score(kb, h), step, first=s0_ref[...], before_last=prefetch_before(kb))
        return 0

    safe = jnp.max(functools.reduce(jnp.maximum, [bounds[h] - m_ref[h] for h in range(n)])) <= SOFTMAX_MARGIN

    @pl.when(safe)
    def _():
        lax.fori_loop(0, qb - lo, unchecked_body, 0)

    @pl.when(jnp.logical_not(safe))
    def _():
        _checked_blocks(n, score, adjust, vblock, scratch, lo, qb, t)

    return [acc_ref[h, 0:HEAD_DIM] / acc_ref[h, HEAD_DIM:HEAD_DIM + 1] for h in range(n)]


def _score_bounds(q_heads, kn_ref, row0, extra):
    out = []
    for h, q in enumerate(q_heads):
        qn2 = jnp.sum(jnp.square(q.astype(F32)), axis=0, keepdims=True)
        kn2 = jnp.max(kn_ref[0, row0 + h:row0 + h + 1, :], axis=1, keepdims=True)
        out.append(jnp.sqrt(qn2 * kn2) * 1.001 + (1.0 + extra))
    return out


def _diff_stick_kernel(qd_ref, kd_ref, vd_ref, kn_ref, lam_ref, g_ref, qs_ref, ks_ref, vs_ref, ut_ref,
                       od_ref, os_ref, s0_ref, m_ref, acc_ref, run_ref, sacc_ref, *, lambda_init):
    qb = pl.program_id(1)
    t = qd_ref.shape[-1]
    sub = t // 2
    n = 2 * N_HEADS
    gd, gs = LANES // DIFF_DIM, LANES // HEAD_DIM
    ones = jnp.ones((ONES_ROWS, t), BF16)
    lam_p = lam_ref[...]
    lam = (jnp.exp(jnp.sum(lam_p[0:1] * lam_p[1:2], keepdims=True))
           - jnp.exp(jnp.sum(lam_p[2:3] * lam_p[3:4], keepdims=True)) + lambda_init)
    qmd = [_masked_q(qd_ref, head, DIFF_DIM) for head in range(n)]
    qms = [_masked_q(qs_ref, head, HEAD_DIM) for head in range(N_HEADS)]
    run_ref[...] = jnp.zeros_like(run_ref)
    sacc_ref[...] = jnp.zeros_like(sacc_ref)

    def d_score(kb, head):
        return _dot(_load_k(kd_ref, kb, head // gd, t), qmd[head])

    def d_adjust(kb, head, s, diag):
        return jnp.where(_causal_tile(t, False), s, NEG_BIG) if diag else s

    def d_vblock(kb, head):
        return _load_v(vd_ref, kb, head // 2)

    def d_prefetch(kb):
        def prefetch():
            s0_ref[...] = d_score(jnp.maximum(kb - 1, 0), 0)
        return prefetch

    def d_first(head):
        def step(s):
            s = d_adjust(qb, head, s, True)
            m = jnp.max(s, axis=0, keepdims=True)
            m_ref[head] = m
            acc_ref[head] = _dot(jnp.concatenate([d_vblock(qb, head), ones], axis=0), jnp.exp2(s - m).astype(BF16))
        return step

    def d_unchecked(kb, head):
        def step(s):
            p = jnp.exp2(s - m_ref[head]).astype(BF16)
            acc_ref[head] += _dot(jnp.concatenate([d_vblock(kb, head), ones], axis=0), p)
        return step

    def d_chains(kb, step_of):
        return [(functools.partial(d_score, kb, head), step_of(head)) for head in range(n)]

    def s_chains(hb, q0, key_off):
        cols = slice(q0, t)
        k0 = pl.multiple_of(hb * sub, sub)
        if key_off is not None:
            key = lax.broadcasted_iota(jnp.int32, (sub, t - q0), 0) + key_off
            qry = lax.broadcasted_iota(jnp.int32, (sub, t - q0), 1) + q0
            valid = key < qry

        stash = {}

        def score_a(head):
            g = head // gs
            return _dot(ks_ref[0, pl.ds(k0, sub), LANES * g:LANES * (g + 1)], qms[head][:, cols])

        def step_a(head, z):
            lk = -(jnp.maximum(z, 0.0) + jnp.log(1.0 + jnp.exp(-jnp.abs(z))))
            lkm = lk if key_off is None else jnp.where(valid, lk, 0.0)
            hi = lkm.astype(BF16)
            stash[head] = (z + lk, lkm[0:1], hi, (lkm - hi.astype(F32)).astype(BF16))

        def score_b(head):
            _, _, hi, lo = stash[head]
            return _dot(ut_ref[...], hi) + _dot(ut_ref[...], lo)

        def step_b(head, within):
            log_sig, lkm0, _, _ = stash[head]
            a = jnp.exp(log_sig + within + run_ref[head, :, cols])
            if key_off is not None:
                a = jnp.where(valid, a, 0.0)
            sacc_ref[head, :, cols] += _dot(vs_ref[0, hb, HEAD_DIM * head:HEAD_DIM * (head + 1), :], a.astype(BF16))
            run_ref[head, :, cols] += within[0:1] + lkm0

        a_chains = [(functools.partial(score_a, head), functools.partial(step_a, head)) for head in range(N_HEADS)]
        b_chains = [(functools.partial(score_b, head), functools.partial(step_b, head)) for head in range(N_HEADS)]
        order = [a_chains[0]]
        for head in range(1, N_HEADS):
            order += [a_chains[head], b_chains[head - 1]]
        return order + [b_chains[N_HEADS - 1]]

    def alive():
        worst = functools.reduce(jnp.maximum, [run_ref[head] for head in range(N_HEADS)])
        return jnp.max(worst) > STICK_EXIT

    _run_chains(_interleave(d_chains(qb, lambda head: d_first(head)),
                            s_chains(2 * qb + 1, sub, sub) + s_chains(2 * qb, 0, 0)),
                before_last=d_prefetch(qb))

    bounds = _score_bounds(qmd, kn_ref, 0, 0.0)
    safe = jnp.max(functools.reduce(jnp.maximum, [bounds[h] - m_ref[h] for h in range(n)])) <= SOFTMAX_MARGIN
    n_fused = jnp.minimum(qb, FUSED_STICK_STEPS)

    @pl.when(safe)
    def _():
        def fused(i, _):
            kb = qb - 1 - i
            _run_chains(_interleave(d_chains(kb, lambda head: d_unchecked(kb, head)),
                                    s_chains(2 * qb - 1 - i, 0, None)),
                        first=s0_ref[...], before_last=d_prefetch(kb))
            return 0

        def plain(i, _):
            kb = qb - 1 - i
            _run_chains(d_chains(kb, lambda head: d_unchecked(kb, head)), first=s0_ref[...],
                        before_last=d_prefetch(kb))
            return 0

        lax.fori_loop(0, n_fused, fused, 0)
        lax.fori_loop(n_fused, qb, plain, 0)

    @pl.when(jnp.logical_not(safe))
    def _():
        _checked_blocks(n, d_score, d_adjust, d_vblock, (s0_ref, m_ref, acc_ref), 0, qb, t)

    def earlier(state):
        i, _ = state
        _run_chains(s_chains(2 * qb - 1 - i, 0, None))
        return i + 1, alive()

    lax.while_loop(lambda st: jnp.logical_and(st[0] < 2 * qb, st[1]), earlier,
                   (jnp.where(safe, n_fused, 0), alive()))
    os_ref[0] = jnp.concatenate([sacc_ref[head] for head in range(N_HEADS)], axis=0).T.astype(BF16)

    comps = [acc_ref[h, 0:HEAD_DIM] / acc_ref[h, HEAD_DIM:HEAD_DIM + 1] for h in range(n)]
    outs = []
    for hh in range(N_HEADS):
        o = comps[2 * hh] - lam * comps[2 * hh + 1]
        o = o * lax.rsqrt(jnp.mean(o * o, axis=0, keepdims=True) + LN_EPS)
        outs.append(o * (g_ref[...] * (1.0 - lambda_init)))
    od_ref[0] = jnp.concatenate(outs, axis=0).T.astype(BF16)


def _dilated_kernel(q_ref, k_ref, v_ref, kn_ref, bias_ref, o_ref, *scratch):
    qb = pl.program_id(1)
    t = q_ref.shape[-1]
    n_back = bias_ref.shape[0] - 1
    per_group = LANES // HEAD_DIM
    qms = [_masked_q(q_ref, head, HEAD_DIM) for head in range(N_HEADS)]

    def score(kb, head):
        return _dot(_load_k(k_ref, kb, head // per_group, t), qms[head])

    def adjust(kb, head, s, diag):
        return bias_ref[qb - kb] + s

    outs = _softmax_blocks(N_HEADS, score, adjust, lambda kb, head: _load_v(v_ref, kb, head), scratch,
                           jnp.maximum(qb - n_back, 0), qb, t,
                           _score_bounds(qms, kn_ref, 2 * N_HEADS, math.log2(len(DILATED_PATTERNS))))
    o_ref[0] = jnp.concatenate(outs, axis=0).T.astype(BF16)


def _forget_kernel(q_ref, k_ref, v_ref, kn_ref, o_ref, *scratch):
    qb = pl.program_id(1)
    t = q_ref.shape[-1]
    qas = [q_ref[0, 0, AUG * head:AUG * (head + 1), :] for head in range(N_HEADS)]

    def score(kb, head):
        return _dot(_load_k(k_ref, kb, head, t), qas[head])

    def adjust(kb, head, s, diag):
        return jnp.where(_causal_tile(t, False), s, NEG_BIG) if diag else s

    bounds = _score_bounds([qa[0:HEAD_DIM] for qa in qas], kn_ref, 3 * N_HEADS, 0.0)
    outs = _softmax_blocks(N_HEADS, score, adjust, lambda kb, head: _load_v(v_ref, kb, head), scratch, 0, qb, t,
                           bounds)
    o_ref[0] = jnp.concatenate(outs, axis=0).T.astype(BF16)


def _dilated_bias(t):
    n_back = max(w for w, _ in DILATED_PATTERNS) // t
    kk = np.arange(t)[:, None]
    qq = np.arange(t)[None, :]
    tiles = []
    for d in range(n_back + 1):
        dist = d * t + qq - kk
        count = np.zeros((t, t), np.float64)
        for window, dil in DILATED_PATTERNS:
            count += (dist >= 0) & (dist % dil == 0) & (dist // dil <= window // dil)
        tiles.append(np.where(count > 0, np.log2(np.maximum(count, 1.0)), NEG_BIG))
    return jnp.asarray(np.stack(tiles), F32)


def _attention(kind, branch, qv, k, extra_inputs, extra_specs, kernel, l, scratch=()):
    b, ns, _, t = qv.shape
    s = ns * t
    n_aug = N_HEADS * AUG
    if branch == 3:
        width, qk_block = n_aug, 0
    else:
        width, qk_block = BRANCH_WIDTH, n_aug // BRANCH_WIDTH + branch
    v_block = (n_aug + 3 * BRANCH_WIDTH) // BRANCH_WIDTH + branch
    in_specs = [
        pl.BlockSpec((1, 1, width, t), lambda i, j: (i, j, qk_block, 0)),
        pl.BlockSpec((1, s, width), lambda i, j: (i, 0, qk_block)),
        pl.BlockSpec((1, ns, BRANCH_WIDTH, t), lambda i, j: (i, 0, v_block, 0)),
    ] + extra_specs
    return pl.pallas_call(
        kernel,
        grid=(b, ns),
        in_specs=in_specs,
        out_specs=pl.BlockSpec((1, t, BRANCH_WIDTH), lambda i, j: (i, j, 0)),
        out_shape=jax.ShapeDtypeStruct((b, s, BRANCH_WIDTH), BF16),
        scratch_shapes=list(scratch),
        compiler_params=_params("arbitrary", "arbitrary"),
        name=f"{kind}_attention_l{l}",
    )(qv, k, qv, *extra_inputs)


def _diff_stick_attention(qv, k, vs, kn, lam, gain, ut, lambda_init, l):
    b, ns, _, t = qv.shape
    s = ns * t
    first = N_HEADS * AUG // BRANCH_WIDTH
    v_block = (N_HEADS * AUG + 3 * BRANCH_WIDTH) // BRANCH_WIDTH
    const = lambda shape: pl.BlockSpec(shape, lambda i, j: (0,) * len(shape))
    q_spec = lambda blk: pl.BlockSpec((1, 1, BRANCH_WIDTH, t), lambda i, j: (i, j, blk, 0))
    k_spec = lambda blk: pl.BlockSpec((1, s, BRANCH_WIDTH), lambda i, j: (i, 0, blk))
    o_spec = pl.BlockSpec((1, t, BRANCH_WIDTH), lambda i, j: (i, j, 0))
    n = 2 * N_HEADS
    return pl.pallas_call(
        functools.partial(_diff_stick_kernel, lambda_init=lambda_init),
        grid=(b, ns),
        in_specs=[q_spec(first), k_spec(first),
                  pl.BlockSpec((1, ns, BRANCH_WIDTH, t), lambda i, j: (i, 0, v_block, 0)),
                  pl.BlockSpec((1, KN_ROWS, s), lambda i, j: (i, 0, 0)), const(lam.shape), const(gain.shape),
                  q_spec(first + 1), k_spec(first + 1),
                  pl.BlockSpec((1, 2 * ns, BRANCH_WIDTH, t // 2), lambda i, j: (i, 0, 0, 0)), const(ut.shape)],
        out_specs=[o_spec, o_spec],
        out_shape=[jax.ShapeDtypeStruct((b, s, BRANCH_WIDTH), BF16)] * 2,
        scratch_shapes=[pltpu.VMEM((t, t), F32), pltpu.VMEM((n, 1, t), F32),
                        pltpu.VMEM((n, HEAD_DIM + ONES_ROWS, t), F32),
                        pltpu.VMEM((N_HEADS, 1, t), F32), pltpu.VMEM((N_HEADS, HEAD_DIM, t), F32)],
        compiler_params=_params("arbitrary", "arbitrary"),
        name=f"diff_stick_attention_l{l}",
    )(qv, k, qv, kn, lam, gain, qv, k, vs, ut)


def _merge_kernel(x_ref, sc_ref, sh_ref, ga_ref, oa_ref, ob_ref, oc_ref, od_ref, wg_ref, wb_ref, wo_ref,
                  lng_ref, lnb_ref, out_ref):
    x = x_ref[0]
    d = x.shape[-1]
    h = (x * (1.0 + sc_ref[0, 0]) + sh_ref[0, 0]).astype(BF16)
    merged = None
    for n, o_ref in enumerate((oa_ref, ob_ref, oc_ref, od_ref)):
        gate = 1.0 / (1.0 + jnp.exp(-_dot(h, wg_ref[:, d * n:d * (n + 1)])))
        y = gate * _dot(o_ref[0], wb_ref[n])
        merged = y if merged is None else merged + y
    mix = _dot(merged.astype(BF16), wo_ref[...])
    out_ref[0] = _layer_norm(DEEPNORM_ALPHA * x + ga_ref[0, 0] * mix, lng_ref[...], lnb_ref[...])


def _merge(x, mod, outs, wg, wb, wo, ln_g, ln_b, l):
    b, s, d = x.shape
    t = SEQ_TILE
    const = lambda shape: pl.BlockSpec(shape, lambda i, j: (0,) * len(shape))
    mod_spec = lambda idx: pl.BlockSpec((1, 1, 1, d), lambda i, j: (i, idx, 0, 0))
    o_spec = pl.BlockSpec((1, t, BRANCH_WIDTH), lambda i, j: (i, j, 0))
    return pl.pallas_call(
        _merge_kernel,
        grid=(b, s // t),
        in_specs=[pl.BlockSpec((1, t, d), lambda i, j: (i, j, 0)), mod_spec(1), mod_spec(0), mod_spec(2),
                  o_spec, o_spec, o_spec, o_spec,
                  const((d, N_BRANCHES * d)), const((N_BRANCHES, BRANCH_WIDTH, d)), const((d, d)),
                  const((1, d)), const((1, d))],
        out_specs=pl.BlockSpec((1, t, d), lambda i, j: (i, j, 0)),
        out_shape=jax.ShapeDtypeStruct((b, s, d), F32),
        compiler_params=_params("arbitrary", "arbitrary"),
        name=f"merge_l{l}",
    )(x, mod, mod, mod, *outs, wg, wb, wo, ln_g.reshape(1, d), ln_b.reshape(1, d))


def _ffn_kernel(x_ref, sc_ref, sh_ref, gf_ref, wup_ref, cw_ref, cb_ref, wdn_ref, lng_ref, lnb_ref,
                out_ref, tail_ref):
    t = x_ref.shape[1]
    chunk = D_FF // 2

    @pl.when(pl.program_id(1) == 0)
    def _():
        tail_ref[...] = jnp.zeros_like(tail_ref)

    x = x_ref[0]
    h = (x * (1.0 + sc_ref[0, 0]) + sh_ref[0, 0]).astype(BF16)
    rows = lax.broadcasted_iota(jnp.int32, (8, chunk), 0)

    def conv(col0):
        cols = slice(col0, col0 + chunk)
        u = _dot(h, wup_ref[:, cols])
        tail = tail_ref[:, cols]
        u1, u2 = pltpu.roll(u, 1, 0), pltpu.roll(u, 2, 0)
        head1 = jnp.where(rows < 1, pltpu.roll(tail, 1, 0), u1[0:8])
        head2 = jnp.where(rows < 2, pltpu.roll(tail, 2, 0), u2[0:8])
        u1 = jnp.concatenate([head1, u1[8:]], axis=0)
        u2 = jnp.concatenate([head2, u2[8:]], axis=0)
        tail_ref[:, cols] = u[t - 8:t]
        return cw_ref[2:3, cols] * u + cw_ref[1:2, cols] * u1 + cw_ref[0:1, cols] * u2 + cb_ref[:, cols]

    ffn = None
    for c in range(D_FF // chunk):
        a = conv(c * chunk)
        g = conv(D_FF + c * chunk)
        act = (a / (1.0 + jnp.exp(-a)) * g).astype(BF16)
        y = _dot(act, wdn_ref[c * chunk:(c + 1) * chunk, :])
        ffn = y if ffn is None else ffn + y
    out_ref[0] = _layer_norm(DEEPNORM_ALPHA * x + gf_ref[0, 0] * ffn, lng_ref[...], lnb_ref[...])


def _conv_ffn(x, mod, wup, cw, cb, wdn, ln_g, ln_b, l):
    b, s, d = x.shape
    t = SEQ_TILE
    const = lambda shape: pl.BlockSpec(shape, lambda i, j: (0,) * len(shape))
    mod_spec = lambda idx: pl.BlockSpec((1, 1, 1, d), lambda i, j: (i, idx, 0, 0))
    return pl.pallas_call(
        _ffn_kernel,
        grid=(b, s // t),
        in_specs=[pl.BlockSpec((1, t, d), lambda i, j: (i, j, 0)), mod_spec(4), mod_spec(3), mod_spec(5),
                  const((d, 2 * D_FF)), const((CONV_WIDTH, 2 * D_FF)), const((1, 2 * D_FF)), const((D_FF, d)),
                  const((1, d)), const((1, d))],
        out_specs=pl.BlockSpec((1, t, d), lambda i, j: (i, j, 0)),
        out_shape=jax.ShapeDtypeStruct((b, s, d), F32),
        scratch_shapes=[pltpu.VMEM((8, 2 * D_FF), F32)],
        compiler_params=_params("arbitrary", "arbitrary"),
        name=f"conv_ffn_l{l}",
    )(x, mod, mod, mod, wup, cw, cb.reshape(1, 2 * D_FF), wdn, ln_g.reshape(1, d), ln_b.reshape(1, d))


def kernel(x, c, positions, w_ada, b_ada, w_in, lam_q1, lam_k1, lam_q2, lam_k2, subln_g, forget_b, w_branch,
           w_o, ln1_g, ln1_b, w_up, conv_w, conv_b, w_down, ln2_g, ln2_b):
    b, s, d = x.shape
    t = SEQ_TILE
    bw = BRANCH_WIDTH
    const = lambda shape: pl.BlockSpec(shape, lambda i, j: (0,) * len(shape))

    mod = _modulation(c, w_ada, b_ada)
    cos, sin = _rope_tables(positions)
    bias = _dilated_bias(t)
    sub = t // 2
    ut = jnp.asarray(np.triu(np.ones((sub, sub), np.float32), 1), BF16)
    chains = lambda n: [pltpu.VMEM((t, t), F32), pltpu.VMEM((n, 1, t), F32),
                        pltpu.VMEM((n, HEAD_DIM + ONES_ROWS, t), F32)]

    for l in range(DEPTH):
        lambda_init = 0.8 - 0.6 * math.exp(-0.3 * l)
        w = w_in[l]
        col = lambda i: w[:, bw * i:bw * (i + 1)]
        wq = jnp.concatenate([col(0) * (DIFF_DIM ** -0.5 * LOG2E), col(3) * HEAD_DIM ** -0.5,
                              col(6) * (HEAD_DIM ** -0.5 * LOG2E), col(9) * (HEAD_DIM ** -0.5 * LOG2E)],
                             axis=1).T.astype(BF16)
        wk = jnp.concatenate([col(1), col(4), col(7), col(10)], axis=1).T.astype(BF16)
        wf = jnp.zeros((d, F_ROWS), F32).at[:, :N_HEADS].set(w[:, 12 * bw:12 * bw + N_HEADS])
        wv = jnp.concatenate([col(2), col(5), col(8), col(11), wf], axis=1).T.astype(BF16)
        wg = w[:, 12 * bw + N_HEADS:].astype(BF16)
        fb = jnp.zeros((F_ROWS, 1), F32).at[:N_HEADS, 0].set(forget_b[l])

        qv, k, vs, kn = _projection(x, mod[l], mod[l], wq, wk, wv, fb, cos, sin, l)

        lam = jnp.stack([lam_q1[l], lam_k1[l], lam_q2[l], lam_k2[l]])
        kn_spec = pl.BlockSpec((1, KN_ROWS, s), lambda i, j: (i, 0, 0))
        oa, ob = _diff_stick_attention(qv, k, vs, kn, lam, subln_g[l].reshape(HEAD_DIM, 1), ut, lambda_init, l)
        oc = _attention("dilated", 2, qv, k, [kn, bias], [kn_spec, const(bias.shape)], _dilated_kernel, l,
                        chains(N_HEADS))
        od = _attention("forget", 3, qv, k, [kn], [kn_spec], _forget_kernel, l, chains(N_HEADS))

        x = _merge(x, mod[l], (oa, ob, oc, od), wg, w_branch[l].astype(BF16), w_o[l].astype(BF16),
                   ln1_g[l], ln1_b[l], l)
        x = _conv_ffn(x, mod[l], w_up[l].astype(BF16), conv_w[l], conv_b[l], w_down[l].astype(BF16),
                      ln2_g[l], ln2_b[l], l)
    return x
```

```python
import functools
import math

import numpy as np
import jax
import jax.numpy as jnp
from jax import lax
from jax.experimental import pallas as pl
from jax.experimental.pallas import tpu as pltpu

D_MODEL = 1024
DEPTH = 2
HEAD_DIM = 64
DIFF_DIM = HEAD_DIM // 2
N_HEADS = 4
BRANCH_WIDTH = N_HEADS * HEAD_DIM
N_BRANCHES = 4
ROPE_THETA = 500000.0
ROPE_FRACTION = 4
DILATED_PATTERNS = ((128, 1), (512, 4), (2048, 16))
D_FF = 2048
CONV_WIDTH = 3
LN_EPS = 1e-5
DEEPNORM_ALPHA = (2.0 * DEPTH) ** 0.25

SEQ_TILE = 512
LANES = 128
F_ROWS = 16
ONES_ROWS = 16
NEG_BIG = -1e30
LOG2E = math.log2(math.e)
SOFTMAX_MARGIN = 64.0
FUSED_STICK_STEPS = 3
STICK_EXIT = -105.0
KN_ROWS = 4 * N_HEADS
AUG = LANES
VMEM_LIMIT = 56 * 2**20

F32 = jnp.float32
BF16 = jnp.bfloat16
_NT = (((1,), (1,)), ((), ()))


def _dot(a, b):
    return jnp.dot(a, b, preferred_element_type=F32)


def _params(*sem):
    return pltpu.CompilerParams(dimension_semantics=sem, vmem_limit_bytes=VMEM_LIMIT)


def _split3(x):
    hi = x.astype(BF16)
    r1 = x - hi.astype(F32)
    mid = r1.astype(BF16)
    lo = (r1 - mid.astype(F32)).astype(BF16)
    return hi, mid, lo


def _layer_norm(r, g, b):
    mu = jnp.mean(r, axis=-1, keepdims=True)
    d = r - mu
    var = jnp.mean(d * d, axis=-1, keepdims=True)
    return d * lax.rsqrt(var + LN_EPS) * g + b


def _mod_kernel(c_ref, w_ref, b_ref, o_ref):
    c = c_ref[...]
    a = c / (1.0 + jnp.exp(-c))
    a_hi = a.astype(BF16)
    a_lo = (a - a_hi.astype(F32)).astype(BF16)
    w = w_ref[0]
    w_hi = w.astype(BF16)
    w_lo = (w - w_hi.astype(F32)).astype(BF16)
    o_ref[0] = _dot(a_hi, w_hi) + _dot(a_lo, w_hi) + _dot(a_hi, w_lo) + b_ref[0]


def _modulation(c, w_ada, b_ada):
    b = c.shape[0]
    rows = 8
    c_pad = jnp.zeros((rows, D_MODEL), F32).at[:b].set(c)
    out = pl.pallas_call(
        _mod_kernel,
        grid=(DEPTH, 6),
        in_specs=[
            pl.BlockSpec((rows, D_MODEL), lambda l, j: (0, 0)),
            pl.BlockSpec((1, D_MODEL, D_MODEL), lambda l, j: (l, 0, j)),
            pl.BlockSpec((1, 1, D_MODEL), lambda l, j: (l, 0, j)),
        ],
        out_specs=pl.BlockSpec((1, rows, D_MODEL), lambda l, j: (l, 0, j)),
        out_shape=jax.ShapeDtypeStruct((DEPTH, rows, 6 * D_MODEL), F32),
        compiler_params=_params("arbitrary", "arbitrary"),
        name="adaln_mod",
    )(c_pad, w_ada, b_ada.reshape(DEPTH, 1, 6 * D_MODEL))
    return out[:, :b].reshape(DEPTH, b, 6, 1, D_MODEL)


def _rope_kernel(pos_ref, invf_ref, sgn_ref, cos_ref, sin_ref):
    ang = invf_ref[...] * pos_ref[0].astype(F32)
    cos_ref[0] = jnp.cos(ang)
    sin_ref[0] = jnp.sin(ang) * sgn_ref[...]


def _rope_tables(positions):
    b, s = positions.shape
    r_a, r_c = DIFF_DIM // ROPE_FRACTION, HEAD_DIM // ROPE_FRACTION
    inv_a = jnp.power(jnp.float32(ROPE_THETA), -2.0 * jnp.arange(r_a // 2, dtype=F32) / r_a)
    inv_c = jnp.power(jnp.float32(ROPE_THETA), -2.0 * jnp.arange(r_c // 2, dtype=F32) / r_c)
    invf = jnp.concatenate([inv_a, inv_a, inv_c]).reshape(16, 1)
    sgn = jnp.concatenate([-jnp.ones(4, F32), jnp.ones(12, F32)]).reshape(16, 1)
    return pl.pallas_call(
        _rope_kernel,
        grid=(b,),
        in_specs=[
            pl.BlockSpec((1, 1, s), lambda i: (i, 0, 0)),
            pl.BlockSpec((16, 1), lambda i: (0, 0)),
            pl.BlockSpec((16, 1), lambda i: (0, 0)),
        ],
        out_specs=[pl.BlockSpec((1, 16, s), lambda i: (i, 0, 0))] * 2,
        out_shape=[jax.ShapeDtypeStruct((b, 16, s), F32)] * 2,
        compiler_params=_params("arbitrary"),
        name="rope_tables",
    )(positions.reshape(b, 1, s), invf, sgn)


def _rope_rows(y, cos, sin):
    cos_a, sin_a, cos_c, sin_c = cos[0:8], sin[0:8], cos[8:16], sin[8:16]
    parts = []
    for hh in range(2 * N_HEADS):
        r0 = DIFF_DIM * hh
        seg = y[r0:r0 + 8]
        parts.append(seg * cos_a + pltpu.roll(seg, 4, 0) * sin_a)
        parts.append(y[r0 + 8:r0 + DIFF_DIM])
    parts.append(y[BRANCH_WIDTH:2 * BRANCH_WIDTH])
    for hh in range(N_HEADS):
        r0 = 2 * BRANCH_WIDTH + HEAD_DIM * hh
        t0, t1 = y[r0:r0 + 8], y[r0 + 8:r0 + 16]
        parts.append(t0 * cos_c - t1 * sin_c)
        parts.append(t1 * cos_c + t0 * sin_c)
        parts.append(y[r0 + 16:r0 + HEAD_DIM])
    parts.append(y[3 * BRANCH_WIDTH:4 * BRANCH_WIDTH])
    return jnp.concatenate(parts, axis=0)


def _proj_kernel(x_ref, sc_ref, sh_ref, wq_ref, wk_ref, wv_ref, fb_ref, cos_ref, sin_ref, u_ref,
                 qv_ref, k_ref, vs_ref, kn_ref, carry_ref):
    t = x_ref.shape[1]
    nq = 4 * BRANCH_WIDTH
    n3 = 3 * BRANCH_WIDTH

    @pl.when(pl.program_id(1) == 0)
    def _():
        carry_ref[...] = jnp.zeros_like(carry_ref)

    h = (x_ref[0] * (1.0 + sc_ref[0, 0]) + sh_ref[0, 0]).astype(BF16)
    cos, sin = cos_ref[0], sin_ref[0]

    yq = _rope_rows(lax.dot_general(wq_ref[...], h, _NT, preferred_element_type=F32), cos, sin)
    yk = _rope_rows(lax.dot_general(wk_ref[...], h, _NT, preferred_element_type=F32), cos, sin)
    yv = lax.dot_general(wv_ref[...], h, _NT, preferred_element_type=F32)

    fl = yv[nq:nq + F_ROWS] + fb_ref[...]
    lf = jnp.minimum(fl, 0.0) - jnp.log(1.0 + jnp.exp(-jnp.abs(fl)))
    cs = _dot(jnp.concatenate(_split3(lf), axis=0), u_ref[...])
    fcum = cs[0:F_ROWS] + cs[F_ROWS:2 * F_ROWS] + cs[2 * F_ROWS:3 * F_ROWS] + carry_ref[:, 0:1]
    carry_ref[...] = jnp.broadcast_to(fcum[:, t - 1:t], carry_ref.shape)

    f2 = fcum * LOG2E
    row8 = lax.broadcasted_iota(jnp.int32, (8, t), 0)
    ones3 = jnp.where(row8 < 3, 1.0, 0.0)
    pad = jnp.zeros((AUG - HEAD_DIM - 16, t), F32)
    q_rows, k_rows = [], []
    for hh in range(N_HEADS):
        hi, mid, lo = (p.astype(F32) for p in _split3(f2[hh:hh + 1]))
        pieces = jnp.where(row8 == 0, hi, jnp.where(row8 == 1, mid, jnp.where(row8 == 2, lo, 0.0)))
        feat = slice(n3 + HEAD_DIM * hh, n3 + HEAD_DIM * (hh + 1))
        q_rows += [yq[feat], ones3, pieces, pad]
        k_rows += [yk[feat], -pieces, ones3, pad]
    qv_ref[0, 0, 0:N_HEADS * AUG + n3, :] = jnp.concatenate(q_rows + [yq[0:n3]], axis=0).astype(BF16)
    qv_ref[0, 0, N_HEADS * AUG + n3:N_HEADS * AUG + n3 + nq, :] = yv[0:nq].astype(BF16)
    ksq = jnp.square(yk.astype(BF16).astype(F32))
    head_rows = ([(DIFF_DIM * i, DIFF_DIM) for i in range(2 * N_HEADS)]
                 + [(2 * BRANCH_WIDTH + HEAD_DIM * i, HEAD_DIM) for i in range(N_HEADS)]
                 + [(n3 + HEAD_DIM * i, HEAD_DIM) for i in range(N_HEADS)])
    kn_ref[0] = jnp.concatenate([jnp.sum(ksq[r0:r0 + w], axis=0, keepdims=True) for r0, w in head_rows], axis=0)
    for half in range(2):
        vs_ref[0, half] = yv[BRANCH_WIDTH:2 * BRANCH_WIDTH, half * (t // 2):(half + 1) * (t // 2)].astype(BF16)
    k_ref[0] = jnp.concatenate(k_rows + [yk[0:n3]], axis=0).T.astype(BF16)


def _projection(x, sc, sh, wq, wk, wv, fb, cos, sin, l):
    b, s, d = x.shape
    t = SEQ_TILE
    ns = s // t
    nq = 4 * BRANCH_WIDTH
    nk = N_HEADS * AUG + 3 * BRANCH_WIDTH
    u = jnp.asarray(np.triu(np.ones((t, t), np.float32)), BF16)
    const = lambda shape: pl.BlockSpec(shape, lambda i, j: (0,) * len(shape))
    return pl.pallas_call(
        _proj_kernel,
        grid=(b, ns),
        in_specs=[
            pl.BlockSpec((1, t, d), lambda i, j: (i, j, 0)),
            pl.BlockSpec((1, 1, 1, d), lambda i, j: (i, 1, 0, 0)),
            pl.BlockSpec((1, 1, 1, d), lambda i, j: (i, 0, 0, 0)),
            const((nq, d)), const((nq, d)), const((nq + F_ROWS, d)),
            const((F_ROWS, 1)),
            pl.BlockSpec((1, 16, t), lambda i, j: (i, 0, j)),
            pl.BlockSpec((1, 16, t), lambda i, j: (i, 0, j)),
            const((t, t)),
        ],
        out_specs=[
            pl.BlockSpec((1, 1, nk + nq, t), lambda i, j: (i, j, 0, 0)),
            pl.BlockSpec((1, t, nk), lambda i, j: (i, j, 0)),
            pl.BlockSpec((1, 2, BRANCH_WIDTH, t // 2), lambda i, j: (i, j, 0, 0)),
            pl.BlockSpec((1, KN_ROWS, t), lambda i, j: (i, 0, j)),
        ],
        out_shape=[
            jax.ShapeDtypeStruct((b, ns, nk + nq, t), BF16),
            jax.ShapeDtypeStruct((b, s, nk), BF16),
            jax.ShapeDtypeStruct((b, 2 * ns, BRANCH_WIDTH, t // 2), BF16),
            jax.ShapeDtypeStruct((b, KN_ROWS, s), F32),
        ],
        scratch_shapes=[pltpu.VMEM((F_ROWS, LANES), F32)],
        compiler_params=_params("arbitrary", "arbitrary"),
        name=f"projection_l{l}",
    )(x, sc, sh, wq, wk, wv, fb, cos, sin, u)


def _masked_q(q_ref, head, width):
    per_group = LANES // width
    g = head // per_group
    qg = q_ref[0, 0, LANES * g:LANES * (g + 1), :]
    rows = lax.broadcasted_iota(jnp.int32, qg.shape, 0)
    r0 = width * (head % per_group)
    return jnp.where((rows >= r0) & (rows < r0 + width), qg, jnp.zeros_like(qg))


def _load_k(k_ref, kb, group, t):
    k0 = pl.multiple_of(kb * t, t)
    return k_ref[0, pl.ds(k0, t), LANES * group:LANES * (group + 1)]


def _load_v(v_ref, kb, head):
    return v_ref[0, kb, HEAD_DIM * head:HEAD_DIM * (head + 1), :]


def _softmax_step(s, carry, vblk):
    m, acc = carry
    t = s.shape[1]
    m_new = jnp.maximum(m, jnp.max(s, axis=0, keepdims=True))
    p = jnp.exp2(s - m_new).astype(BF16)
    alpha = jnp.exp2(m - m_new)
    vaug = jnp.concatenate([vblk, jnp.ones((ONES_ROWS, t), BF16)], axis=0)
    return m_new, alpha * acc + _dot(vaug, p)


def _causal_tile(t, strict):
    key = lax.broadcasted_iota(jnp.int32, (t, t), 0)
    qry = lax.broadcasted_iota(jnp.int32, (t, t), 1)
    return key < qry if strict else key <= qry


def _staggered(n, score_fn, step_fn, first=None, before_last=None):
    out = []
    s_next = score_fn(0) if first is None else first
    for h in range(n):
        s = s_next
        if h + 1 < n:
            s_next = score_fn(h + 1)
        elif before_last is not None:
            before_last()
        out.append(step_fn(h, s))
    return tuple(out)


def _run_chains(chains, first=None, before_last=None):
    _staggered(len(chains), lambda c: chains[c][0](), lambda c, s: chains[c][1](s), first, before_last)


def _interleave(major, minor):
    out, placed = [], 0
    for i, chain in enumerate(major):
        out.append(chain)
        upto = (i + 1) * len(minor) // len(major)
        out += minor[placed:upto]
        placed = upto
    return out


def _checked_blocks(n, score, adjust, vblock, scratch, lo, qb, t):
    s0_ref, m_ref, acc_ref = scratch
    ones = jnp.ones((ONES_ROWS, t), BF16)

    def prefetch_before(kb):
        def prefetch():
            s0_ref[...] = score(jnp.maximum(kb - 1, 0), 0)
        return prefetch

    def loop_body(i, _):
        kb = qb - 1 - i
        excess = []

        def fast_step(h, s):
            m = m_ref[h]
            s = adjust(kb, h, s, False)
            excess.append(jnp.max(s, axis=0, keepdims=True) - m)
            return _dot(jnp.concatenate([vblock(kb, h), ones], axis=0), jnp.exp2(s - m).astype(BF16))

        deltas = _staggered(n, lambda h: score(kb, h), fast_step, first=s0_ref[...],
                            before_last=prefetch_before(kb))
        over = jnp.max(functools.reduce(jnp.maximum, excess)) > SOFTMAX_MARGIN

        @pl.when(jnp.logical_not(over))
        def _():
            for h in range(n):
                acc_ref[h] += deltas[h]

        @pl.when(over)
        def _():
            def exact_step(h, s):
                m_ref[h], acc_ref[h] = _softmax_step(adjust(kb, h, s, False), (m_ref[h], acc_ref[h]), vblock(kb, h))

            _staggered(n, lambda h: score(kb, h), exact_step)

        return 0

    lax.fori_loop(0, qb - lo, loop_body, 0)


def _softmax_blocks(n, score, adjust, vblock, scratch, lo, qb, t, bounds):
    s0_ref, m_ref, acc_ref = scratch
    ones = jnp.ones((ONES_ROWS, t), BF16)

    def prefetch_before(kb):
        def prefetch():
            s0_ref[...] = score(jnp.maximum(kb - 1, 0), 0)
        return prefetch

    def first_step(h, s):
        s = adjust(qb, h, s, True)
        m = jnp.max(s, axis=0, keepdims=True)
        m_ref[h] = m
        acc_ref[h] = _dot(jnp.concatenate([vblock(qb, h), ones], axis=0), jnp.exp2(s - m).astype(BF16))

    _staggered(n, lambda h: score(qb, h), first_step, before_last=prefetch_before(qb))

    def unchecked_body(i, _):
        kb = qb - 1 - i

        def step(h, s):
            p = jnp.exp2(adjust(kb, h, s, False) - m_ref[h]).astype(BF16)
            acc_ref[h] += _dot(jnp.concatenate([vblock(kb, h), ones], axis=0), p)

        _staggered(n, lambda h: score(kb, h), step, first=s0_ref[...], before_last=prefetch_before(kb))
        return 0

    safe = jnp.max(functools.reduce(jnp.maximum, [bounds[h] - m_ref[h] for h in range(n)])) <= SOFTMAX_MARGIN

    @pl.when(safe)
    def _():
        lax.fori_loop(0, qb - lo, unchecked_body, 0)

    @pl.when(jnp.logical_not(safe))
    def _():
        _checked_blocks(n, score, adjust, vblock, scratch, lo, qb, t)

    return [acc_ref[h, 0:HEAD_DIM] / acc_ref[h, HEAD_DIM:HEAD_DIM + 1] for h in range(n)]


def _score_bounds(q_heads, kn_ref, row0, extra):
    out = []
    for h, q in enumerate(q_heads):
        qn2 = jnp.sum(jnp.square(q.astype(F32)), axis=0, keepdims=True)
        kn2 = jnp.max(kn_ref[0, row0 + h:row0 + h + 1, :], axis=1, keepdims=True)
        out.append(jnp.sqrt(qn2 * kn2) * 1.001 + (1.0 + extra))
    return out


def _diff_stick_kernel(qd_ref, kd_ref, vd_ref, kn_ref, lam_ref, g_ref, qs_ref, ks_ref, vs_ref, ut_ref,
                       od_ref, os_ref, s0_ref, m_ref, acc_ref, run_ref, sacc_ref, *, lambda_init):
    qb = pl.program_id(1)
    t = qd_ref.shape[-1]
    sub = t // 2
    n = 2 * N_HEADS
    gd, gs = LANES // DIFF_DIM, LANES // HEAD_DIM
    ones = jnp.ones((ONES_ROWS, t), BF16)
    lam_p = lam_ref[...]
    lam = (jnp.exp(jnp.sum(lam_p[0:1] * lam_p[1:2], keepdims=True))
           - jnp.exp(jnp.sum(lam_p[2:3] * lam_p[3:4], keepdims=True)) + lambda_init)
    qmd = [_masked_q(qd_ref, head, DIFF_DIM) for head in range(n)]
    qms = [_masked_q(qs_ref, head, HEAD_DIM) for head in range(N_HEADS)]
    run_ref[...] = jnp.zeros_like(run_ref)
    sacc_ref[...] = jnp.zeros_like(sacc_ref)

    def d_score(kb, head):
        return _dot(_load_k(kd_ref, kb, head // gd, t), qmd[head])

    def d_adjust(kb, head, s, diag):
        return jnp.where(_causal_tile(t, False), s, NEG_BIG) if diag else s

    def d_vblock(kb, head):
        return _load_v(vd_ref, kb, head // 2)

    def d_prefetch(kb):
        def prefetch():
            s0_ref[...] = d_score(jnp.maximum(kb - 1, 0), 0)
        return prefetch

    def d_first(head):
        def step(s):
            s = d_adjust(qb, head, s, True)
            m = jnp.max(s, axis=0, keepdims=True)
            m_ref[head] = m
            acc_ref[head] = _dot(jnp.concatenate([d_vblock(qb, head), ones], axis=0), jnp.exp2(s - m).astype(BF16))
        return step

    def d_unchecked(kb, head):
        def step(s):
            p = jnp.exp2(s - m_ref[head]).astype(BF16)
            acc_ref[head] += _dot(jnp.concatenate([d_vblock(kb, head), ones], axis=0), p)
        return step

    def d_chains(kb, step_of):
        return [(functools.partial(d_score, kb, head), step_of(head)) for head in range(n)]

    def s_chains(hb, q0, key_off):
        cols = slice(q0, t)
        k0 = pl.multiple_of(hb * sub, sub)
        if key_off is not None:
            key = lax.broadcasted_iota(jnp.int32, (sub, t - q0), 0) + key_off
            qry = lax.broadcasted_iota(jnp.int32, (sub, t - q0), 1) + q0
            valid = key < qry

        stash = {}

        def score_a(head):
            g = head // gs
            return _dot(ks_ref[0, pl.ds(k0, sub), LANES * g:LANES * (g + 1)], qms[head][:, cols])

        def step_a(head, z):
            keep = jnp.maximum(z, 0.0) + jnp.log(1.0 + jnp.exp2(jnp.abs(z) * -LOG2E))
            if key_off is not None:
                keep = jnp.where(valid, keep, 0.0)
            keep = keep.astype(BF16)
            stash[head] = (z - keep.astype(F32), keep)

        def score_b(head):
            return _dot(ut_ref[...], stash[head][1])

        def step_b(head, later):
            log_sig, keep = stash[head]
            a = jnp.exp(log_sig - later + run_ref[head, :, cols])
            if key_off is not None:
                a = jnp.where(valid, a, 0.0)
            sacc_ref[head, :, cols] += _dot(vs_ref[0, hb, HEAD_DIM * head:HEAD_DIM * (head + 1), :], a.astype(BF16))
            run_ref[head, :, cols] -= later[0:1] + keep[0:1].astype(F32)

        a_chains = [(functools.partial(score_a, head), functools.partial(step_a, head)) for head in range(N_HEADS)]
        b_chains = [(functools.partial(score_b, head), functools.partial(step_b, head)) for head in range(N_HEADS)]
        order = [a_chains[0]]
        for head in range(1, N_HEADS):
            order += [a_chains[head], b_chains[head - 1]]
        return order + [b_chains[N_HEADS - 1]]

    def alive():
        worst = functools.reduce(jnp.maximum, [run_ref[head] for head in range(N_HEADS)])
        return jnp.max(worst) > STICK_EXIT

    _run_chains(_interleave(d_chains(qb, lambda head: d_first(head)),
                            s_chains(2 * qb + 1, sub, sub) + s_chains(2 * qb, 0, 0)),
                before_last=d_prefetch(qb))

    bounds = _score_bounds(qmd, kn_ref, 0, 0.0)
    safe = jnp.max(functools.reduce(jnp.maximum, [bounds[h] - m_ref[h] for h in range(n)])) <= SOFTMAX_MARGIN
    n_fused = jnp.minimum(qb, FUSED_STICK_STEPS)

    @pl.when(safe)
    def _():
        def fused(i, _):
            kb = qb - 1 - i
            _run_chains(_interleave(d_chains(kb, lambda head: d_unchecked(kb, head)),
                                    s_chains(2 * qb - 1 - i, 0, None)),
                        first=s0_ref[...], before_last=d_prefetch(kb))
            return 0

        def plain(i, _):
            kb = qb - 1 - i
            _run_chains(d_chains(kb, lambda head: d_unchecked(kb, head)), first=s0_ref[...],
                        before_last=d_prefetch(kb))
            return 0

        lax.fori_loop(0, n_fused, fused, 0)
        lax.fori_loop(n_fused, qb, plain, 0)

    @pl.when(jnp.logical_not(safe))
    def _():
        _checked_blocks(n, d_score, d_adjust, d_vblock, (s0_ref, m_ref, acc_ref), 0, qb, t)

    def earlier(state):
        i, _ = state
        _run_chains(s_chains(2 * qb - 1 - i, 0, None))
        return i + 1, alive()

    lax.while_loop(lambda st: jnp.logical_and(st[0] < 2 * qb, st[1]), earlier,
                   (jnp.where(safe, n_fused, 0), alive()))
    os_ref[0] = jnp.concatenate([sacc_ref[head] for head in range(N_HEADS)], axis=0).T.astype(BF16)

    comps = [acc_ref[h, 0:HEAD_DIM] / acc_ref[h, HEAD_DIM:HEAD_DIM + 1] for h in range(n)]
    outs = []
    for hh in range(N_HEADS):
        o = comps[2 * hh] - lam * comps[2 * hh + 1]
        o = o * lax.rsqrt(jnp.mean(o * o, axis=0, keepdims=True) + LN_EPS)
        outs.append(o * (g_ref[...] * (1.0 - lambda_init)))
    od_ref[0] = jnp.concatenate(outs, axis=0).T.astype(BF16)


def _dilated_kernel(q_ref, k_ref, v_ref, kn_ref, bias_ref, o_ref, *scratch):
    qb = pl.program_id(1)
    t = q_ref.shape[-1]
    n_back = bias_ref.shape[0] - 1
    per_group = LANES // HEAD_DIM
    qms = [_masked_q(q_ref, head, HEAD_DIM) for head in range(N_HEADS)]

    def score(kb, head):
        return _dot(_load_k(k_ref, kb, head // per_group, t), qms[head])

    def adjust(kb, head, s, diag):
        return bias_ref[qb - kb] + s

    outs = _softmax_blocks(N_HEADS, score, adjust, lambda kb, head: _load_v(v_ref, kb, head), scratch,
                           jnp.maximum(qb - n_back, 0), qb, t,
                           _score_bounds(qms, kn_ref, 2 * N_HEADS, math.log2(len(DILATED_PATTERNS))))
    o_ref[0] = jnp.concatenate(outs, axis=0).T.astype(BF16)


def _forget_kernel(q_ref, k_ref, v_ref, kn_ref, o_ref, *scratch):
    qb = pl.program_id(1)
    t = q_ref.shape[-1]
    qas = [q_ref[0, 0, AUG * head:AUG * (head + 1), :] for head in range(N_HEADS)]

    def score(kb, head):
        return _dot(_load_k(k_ref, kb, head, t), qas[head])

    def adjust(kb, head, s, diag):
        return jnp.where(_causal_tile(t, False), s, NEG_BIG) if diag else s

    bounds = _score_bounds([qa[0:HEAD_DIM] for qa in qas], kn_ref, 3 * N_HEADS, 0.0)
    outs = _softmax_blocks(N_HEADS, score, adjust, lambda kb, head: _load_v(v_ref, kb, head), scratch, 0, qb, t,
                           bounds)
    o_ref[0] = jnp.concatenate(outs, axis=0).T.astype(BF16)


def _dilated_bias(t):
    n_back = max(w for w, _ in DILATED_PATTERNS) // t
    kk = np.arange(t)[:, None]
    qq = np.arange(t)[None, :]
    tiles = []
    for d in range(n_back + 1):
        dist = d * t + qq - kk
        count = np.zeros((t, t), np.float64)
        for window, dil in DILATED_PATTERNS:
            count += (dist >= 0) & (dist % dil == 0) & (dist // dil <= window // dil)
        tiles.append(np.where(count > 0, np.log2(np.maximum(count, 1.0)), NEG_BIG))
    return jnp.asarray(np.stack(tiles), F32)


def _attention(kind, branch, qv, k, extra_inputs, extra_specs, kernel, l, scratch=()):
    b, ns, _, t = qv.shape
    s = ns * t
    n_aug = N_HEADS * AUG
    if branch == 3:
        width, qk_block = n_aug, 0
    else:
        width, qk_block = BRANCH_WIDTH, n_aug // BRANCH_WIDTH + branch
    v_block = (n_aug + 3 * BRANCH_WIDTH) // BRANCH_WIDTH + branch
    in_specs = [
        pl.BlockSpec((1, 1, width, t), lambda i, j: (i, j, qk_block, 0)),
        pl.BlockSpec((1, s, width), lambda i, j: (i, 0, qk_block)),
        pl.BlockSpec((1, ns, BRANCH_WIDTH, t), lambda i, j: (i, 0, v_block, 0)),
    ] + extra_specs
    return pl.pallas_call(
        kernel,
        grid=(b, ns),
        in_specs=in_specs,
        out_specs=pl.BlockSpec((1, t, BRANCH_WIDTH), lambda i, j: (i, j, 0)),
        out_shape=jax.ShapeDtypeStruct((b, s, BRANCH_WIDTH), BF16),
        scratch_shapes=list(scratch),
        compiler_params=_params("arbitrary", "arbitrary"),
        name=f"{kind}_attention_l{l}",
    )(qv, k, qv, *extra_inputs)


def _diff_stick_attention(qv, k, vs, kn, lam, gain, ut, lambda_init, l):
    b, ns, _, t = qv.shape
    s = ns * t
    first = N_HEADS * AUG // BRANCH_WIDTH
    v_block = (N_HEADS * AUG + 3 * BRANCH_WIDTH) // BRANCH_WIDTH
    const = lambda shape: pl.BlockSpec(shape, lambda i, j: (0,) * len(shape))
    q_spec = lambda blk: pl.BlockSpec((1, 1, BRANCH_WIDTH, t), lambda i, j: (i, j, blk, 0))
    k_spec = lambda blk: pl.BlockSpec((1, s, BRANCH_WIDTH), lambda i, j: (i, 0, blk))
    o_spec = pl.BlockSpec((1, t, BRANCH_WIDTH), lambda i, j: (i, j, 0))
    n = 2 * N_HEADS
    return pl.pallas_call(
        functools.partial(_diff_stick_kernel, lambda_init=lambda_init),
        grid=(b, ns),
        in_specs=[q_spec(first), k_spec(first),
                  pl.BlockSpec((1, ns, BRANCH_WIDTH, t), lambda i, j: (i, 0, v_block, 0)),
                  pl.BlockSpec((1, KN_ROWS, s), lambda i, j: (i, 0, 0)), const(lam.shape), const(gain.shape),
                  q_spec(first + 1), k_spec(first + 1),
                  pl.BlockSpec((1, 2 * ns, BRANCH_WIDTH, t // 2), lambda i, j: (i, 0, 0, 0)), const(ut.shape)],
        out_specs=[o_spec, o_spec],
        out_shape=[jax.ShapeDtypeStruct((b, s, BRANCH_WIDTH), BF16)] * 2,
        scratch_shapes=[pltpu.VMEM((t, t), F32), pltpu.VMEM((n, 1, t), F32),
                        pltpu.VMEM((n, HEAD_DIM + ONES_ROWS, t), F32),
                        pltpu.VMEM((N_HEADS, 1, t), F32), pltpu.VMEM((N_HEADS, HEAD_DIM, t), F32)],
        compiler_params=_params("arbitrary", "arbitrary"),
        name=f"diff_stick_attention_l{l}",
    )(qv, k, qv, kn, lam, gain, qv, k, vs, ut)


def _merge_kernel(x_ref, sc_ref, sh_ref, ga_ref, oa_ref, ob_ref, oc_ref, od_ref, wg_ref, wb_ref, wo_ref,
                  lng_ref, lnb_ref, out_ref):
    x = x_ref[0]
    d = x.shape[-1]
    h = (x * (1.0 + sc_ref[0, 0]) + sh_ref[0, 0]).astype(BF16)
    merged = None
    for n, o_ref in enumerate((oa_ref, ob_ref, oc_ref, od_ref)):
        gate = 1.0 / (1.0 + jnp.exp(-_dot(h, wg_ref[:, d * n:d * (n + 1)])))
        y = gate * _dot(o_ref[0], wb_ref[n])
        merged = y if merged is None else merged + y
    mix = _dot(merged.astype(BF16), wo_ref[...])
    out_ref[0] = _layer_norm(DEEPNORM_ALPHA * x + ga_ref[0, 0] * mix, lng_ref[...], lnb_ref[...])


def _merge(x, mod, outs, wg, wb, wo, ln_g, ln_b, l):
    b, s, d = x.shape
    t = SEQ_TILE
    const = lambda shape: pl.BlockSpec(shape, lambda i, j: (0,) * len(shape))
    mod_spec = lambda idx: pl.BlockSpec((1, 1, 1, d), lambda i, j: (i, idx, 0, 0))
    o_spec = pl.BlockSpec((1, t, BRANCH_WIDTH), lambda i, j: (i, j, 0))
    return pl.pallas_call(
        _merge_kernel,
        grid=(b, s // t),
        in_specs=[pl.BlockSpec((1, t, d), lambda i, j: (i, j, 0)), mod_spec(1), mod_spec(0), mod_spec(2),
                  o_spec, o_spec, o_spec, o_spec,
                  const((d, N_BRANCHES * d)), const((N_BRANCHES, BRANCH_WIDTH, d)), const((d, d)),
                  const((1, d)), const((1, d))],
        out_specs=pl.BlockSpec((1, t, d), lambda i, j: (i, j, 0)),
        out_shape=jax.ShapeDtypeStruct((b, s, d), F32),
        compiler_params=_params("arbitrary", "arbitrary"),
        name=f"merge_l{l}",
    )(x, mod, mod, mod, *outs, wg, wb, wo, ln_g.reshape(1, d), ln_b.reshape(1, d))


def _ffn_kernel(x_ref, sc_ref, sh_ref, gf_ref, wup_ref, cw_ref, cb_ref, wdn_ref, lng_ref, lnb_ref,
                out_ref, tail_ref):
    t = x_ref.shape[1]
    chunk = D_FF // 2

    @pl.when(pl.program_id(1) == 0)
    def _():
        tail_ref[...] = jnp.zeros_like(tail_ref)

    x = x_ref[0]
    h = (x * (1.0 + sc_ref[0, 0]) + sh_ref[0, 0]).astype(BF16)
    rows = lax.broadcasted_iota(jnp.int32, (8, chunk), 0)

    def conv(col0):
        cols = slice(col0, col0 + chunk)
        u = _dot(h, wup_ref[:, cols])
        tail = tail_ref[:, cols]
        u1, u2 = pltpu.roll(u, 1, 0), pltpu.roll(u, 2, 0)
        head1 = jnp.where(rows < 1, pltpu.roll(tail, 1, 0), u1[0:8])
        head2 = jnp.where(rows < 2, pltpu.roll(tail, 2, 0), u2[0:8])
        u1 = jnp.concatenate([head1, u1[8:]], axis=0)
        u2 = jnp.concatenate([head2, u2[8:]], axis=0)
        tail_ref[:, cols] = u[t - 8:t]
        return cw_ref[2:3, cols] * u + cw_ref[1:2, cols] * u1 + cw_ref[0:1, cols] * u2 + cb_ref[:, cols]

    ffn = None
    for c in range(D_FF // chunk):
        a = conv(c * chunk)
        g = conv(D_FF + c * chunk)
        act = (a / (1.0 + jnp.exp(-a)) * g).astype(BF16)
        y = _dot(act, wdn_ref[c * chunk:(c + 1) * chunk, :])
        ffn = y if ffn is None else ffn + y
    out_ref[0] = _layer_norm(DEEPNORM_ALPHA * x + gf_ref[0, 0] * ffn, lng_ref[...], lnb_ref[...])


def _conv_ffn(x, mod, wup, cw, cb, wdn, ln_g, ln_b, l):
    b, s, d = x.shape
    t = SEQ_TILE
    const = lambda shape: pl.BlockSpec(shape, lambda i, j: (0,) * len(shape))
    mod_spec = lambda idx: pl.BlockSpec((1, 1, 1, d), lambda i, j: (i, idx, 0, 0))
    return pl.pallas_call(
        _ffn_kernel,
        grid=(b, s // t),
        in_specs=[pl.BlockSpec((1, t, d), lambda i, j: (i, j, 0)), mod_spec(4), mod_spec(3), mod_spec(5),
                  const((d, 2 * D_FF)), const((CONV_WIDTH, 2 * D_FF)), const((1, 2 * D_FF)), const((D_FF, d)),
                  const((1, d)), const((1, d))],
        out_specs=pl.BlockSpec((1, t, d), lambda i, j: (i, j, 0)),
        out_shape=jax.ShapeDtypeStruct((b, s, d), F32),
        scratch_shapes=[pltpu.VMEM((8, 2 * D_FF), F32)],
        compiler_params=_params("arbitrary", "arbitrary"),
        name=f"conv_ffn_l{l}",
    )(x, mod, mod, mod, wup, cw, cb.reshape(1, 2 * D_FF), wdn, ln_g.reshape(1, d), ln_b.reshape(1, d))


def kernel(x, c, positions, w_ada, b_ada, w_in, lam_q1, lam_k1, lam_q2, lam_k2, subln_g, forget_b, w_branch,
           w_o, ln1_g, ln1_b, w_up, conv_w, conv_b, w_down, ln2_g, ln2_b):
    b, s, d = x.shape
    t = SEQ_TILE
    bw = BRANCH_WIDTH
    const = lambda shape: pl.BlockSpec(shape, lambda i, j: (0,) * len(shape))

    mod = _modulation(c, w_ada, b_ada)
    cos, sin = _rope_tables(positions)
    bias = _dilated_bias(t)
    sub = t // 2
    ut = jnp.asarray(np.triu(np.ones((sub, sub), np.float32), 1), BF16)
    chains = lambda n: [pltpu.VMEM((t, t), F32), pltpu.VMEM((n, 1, t), F32),
                        pltpu.VMEM((n, HEAD_DIM + ONES_ROWS, t), F32)]

    for l in range(DEPTH):
        lambda_init = 0.8 - 0.6 * math.exp(-0.3 * l)
        w = w_in[l]
        col = lambda i: w[:, bw * i:bw * (i + 1)]
        wq = jnp.concatenate([col(0) * (DIFF_DIM ** -0.5 * LOG2E), col(3) * HEAD_DIM ** -0.5,
                              col(6) * (HEAD_DIM ** -0.5 * LOG2E), col(9) * (HEAD_DIM ** -0.5 * LOG2E)],
                             axis=1).T.astype(BF16)
        wk = jnp.concatenate([col(1), col(4), col(7), col(10)], axis=1).T.astype(BF16)
        wf = jnp.zeros((d, F_ROWS), F32).at[:, :N_HEADS].set(w[:, 12 * bw:12 * bw + N_HEADS])
        wv = jnp.concatenate([col(2), col(5), col(8), col(11), wf], axis=1).T.astype(BF16)
        wg = w[:, 12 * bw + N_HEADS:].astype(BF16)
        fb = jnp.zeros((F_ROWS, 1), F32).at[:N_HEADS, 0].set(forget_b[l])

        qv, k, vs, kn = _projection(x, mod[l], mod[l], wq, wk, wv, fb, cos, sin, l)

        lam = jnp.stack([lam_q1[l], lam_k1[l], lam_q2[l], lam_k2[l]])
        kn_spec = pl.BlockSpec((1, KN_ROWS, s), lambda i, j: (i, 0, 0))
        oa, ob = _diff_stick_attention(qv, k, vs, kn, lam, subln_g[l].reshape(HEAD_DIM, 1), ut, lambda_init, l)
        oc = _attention("dilated", 2, qv, k, [kn, bias], [kn_spec, const(bias.shape)], _dilated_kernel, l,
                        chains(N_HEADS))
        od = _attention("forget", 3, qv, k, [kn], [kn_spec], _forget_kernel, l, chains(N_HEADS))

        x = _merge(x, mod[l], (oa, ob, oc, od), wg, w_branch[l].astype(BF16), w_o[l].astype(BF16),
                   ln1_g[l], ln1_b[l], l)
        x = _conv_ffn(x, mod[l], w_up[l].astype(BF16), conv_w[l], conv_b[l], w_down[l].astype(BF16),
                      ln2_g[l], ln2_b[l], l)
    return x
```

```python
import functools
import math

import numpy as np
import jax
import jax.numpy as jnp
from jax import lax
from jax.experimental import pallas as pl
from jax.experimental.pallas import tpu as pltpu

D_MODEL = 1024
DEPTH = 2
HEAD_DIM = 64
DIFF_DIM = HEAD_DIM // 2
N_HEADS = 4
BRANCH_WIDTH = N_HEADS * HEAD_DIM
N_BRANCHES = 4
ROPE_THETA = 500000.0
ROPE_FRACTION = 4
DILATED_PATTERNS = ((128, 1), (512, 4), (2048, 16))
D_FF = 2048
CONV_WIDTH = 3
LN_EPS = 1e-5
DEEPNORM_ALPHA = (2.0 * DEPTH) ** 0.25

SEQ_TILE = 512
LANES = 128
F_ROWS = 16
ONES_ROWS = 16
NEG_BIG = -1e30
LOG2E = math.log2(math.e)
SOFTMAX_MARGIN = 64.0
FUSED_STICK_STEPS = 3
STICK_EXIT = -105.0
ROW_GROUPS = 2
KN_ROWS = 4 * N_HEADS
AUG = LANES
VMEM_LIMIT = 56 * 2**20

F32 = jnp.float32
BF16 = jnp.bfloat16
_NT = (((1,), (1,)), ((), ()))


def _dot(a, b):
    return jnp.dot(a, b, preferred_element_type=F32)


def _params(*sem):
    return pltpu.CompilerParams(dimension_semantics=sem, vmem_limit_bytes=VMEM_LIMIT)


def _split3(x):
    hi = x.astype(BF16)
    r1 = x - hi.astype(F32)
    mid = r1.astype(BF16)
    lo = (r1 - mid.astype(F32)).astype(BF16)
    return hi, mid, lo


def _layer_norm(r, g, b):
    mu = jnp.mean(r, axis=-1, keepdims=True)
    d = r - mu
    var = jnp.mean(d * d, axis=-1, keepdims=True)
    return d * lax.rsqrt(var + LN_EPS) * g + b


def _mod_kernel(c_ref, w_ref, b_ref, o_ref):
    c = c_ref[...]
    a = c / (1.0 + jnp.exp(-c))
    a_hi = a.astype(BF16)
    a_lo = (a - a_hi.astype(F32)).astype(BF16)
    w = w_ref[0]
    w_hi = w.astype(BF16)
    w_lo = (w - w_hi.astype(F32)).astype(BF16)
    o_ref[0] = _dot(a_hi, w_hi) + _dot(a_lo, w_hi) + _dot(a_hi, w_lo) + b_ref[0]


def _modulation(c, w_ada, b_ada):
    b = c.shape[0]
    rows = 8
    c_pad = jnp.zeros((rows, D_MODEL), F32).at[:b].set(c)
    out = pl.pallas_call(
        _mod_kernel,
        grid=(DEPTH, 6),
        in_specs=[
            pl.BlockSpec((rows, D_MODEL), lambda l, j: (0, 0)),
            pl.BlockSpec((1, D_MODEL, D_MODEL), lambda l, j: (l, 0, j)),
            pl.BlockSpec((1, 1, D_MODEL), lambda l, j: (l, 0, j)),
        ],
        out_specs=pl.BlockSpec((1, rows, D_MODEL), lambda l, j: (l, 0, j)),
        out_shape=jax.ShapeDtypeStruct((DEPTH, rows, 6 * D_MODEL), F32),
        compiler_params=_params("arbitrary", "arbitrary"),
        name="adaln_mod",
    )(c_pad, w_ada, b_ada.reshape(DEPTH, 1, 6 * D_MODEL))
    return out[:, :b].reshape(DEPTH, b, 6, 1, D_MODEL)


def _rope_kernel(pos_ref, invf_ref, sgn_ref, cos_ref, sin_ref):
    ang = invf_ref[...] * pos_ref[0].astype(F32)
    cos_ref[0] = jnp.cos(ang)
    sin_ref[0] = jnp.sin(ang) * sgn_ref[...]


def _rope_tables(positions):
    b, s = positions.shape
    r_a, r_c = DIFF_DIM // ROPE_FRACTION, HEAD_DIM // ROPE_FRACTION
    inv_a = jnp.power(jnp.float32(ROPE_THETA), -2.0 * jnp.arange(r_a // 2, dtype=F32) / r_a)
    inv_c = jnp.power(jnp.float32(ROPE_THETA), -2.0 * jnp.arange(r_c // 2, dtype=F32) / r_c)
    invf = jnp.concatenate([inv_a, inv_a, inv_c]).reshape(16, 1)
    sgn = jnp.concatenate([-jnp.ones(4, F32), jnp.ones(12, F32)]).reshape(16, 1)
    return pl.pallas_call(
        _rope_kernel,
        grid=(b,),
        in_specs=[
            pl.BlockSpec((1, 1, s), lambda i: (i, 0, 0)),
            pl.BlockSpec((16, 1), lambda i: (0, 0)),
            pl.BlockSpec((16, 1), lambda i: (0, 0)),
        ],
        out_specs=[pl.BlockSpec((1, 16, s), lambda i: (i, 0, 0))] * 2,
        out_shape=[jax.ShapeDtypeStruct((b, 16, s), F32)] * 2,
        compiler_params=_params("arbitrary"),
        name="rope_tables",
    )(positions.reshape(b, 1, s), invf, sgn)


def _rope_rows(y, cos, sin):
    cos_a, sin_a, cos_c, sin_c = cos[0:8], sin[0:8], cos[8:16], sin[8:16]
    parts = []
    for hh in range(2 * N_HEADS):
        r0 = DIFF_DIM * hh
        seg = y[r0:r0 + 8]
        parts.append(seg * cos_a + pltpu.roll(seg, 4, 0) * sin_a)
        parts.append(y[r0 + 8:r0 + DIFF_DIM])
    parts.append(y[BRANCH_WIDTH:2 * BRANCH_WIDTH])
    for hh in range(N_HEADS):
        r0 = 2 * BRANCH_WIDTH + HEAD_DIM * hh
        t0, t1 = y[r0:r0 + 8], y[r0 + 8:r0 + 16]
        parts.append(t0 * cos_c - t1 * sin_c)
        parts.append(t1 * cos_c + t0 * sin_c)
        parts.append(y[r0 + 16:r0 + HEAD_DIM])
    parts.append(y[3 * BRANCH_WIDTH:4 * BRANCH_WIDTH])
    return jnp.concatenate(parts, axis=0)


def _proj_kernel(x_ref, sc_ref, sh_ref, wq_ref, wk_ref, wv_ref, fb_ref, cos_ref, sin_ref, u_ref,
                 qv_ref, k_ref, vs_ref, kn_ref, carry_ref):
    t = x_ref.shape[1]
    nq = 4 * BRANCH_WIDTH
    n3 = 3 * BRANCH_WIDTH

    @pl.when(pl.program_id(1) == 0)
    def _():
        carry_ref[...] = jnp.zeros_like(carry_ref)

    h = (x_ref[0] * (1.0 + sc_ref[0, 0]) + sh_ref[0, 0]).astype(BF16)
    cos, sin = cos_ref[0], sin_ref[0]

    yq = _rope_rows(lax.dot_general(wq_ref[...], h, _NT, preferred_element_type=F32), cos, sin)
    yk = _rope_rows(lax.dot_general(wk_ref[...], h, _NT, preferred_element_type=F32), cos, sin)
    yv = lax.dot_general(wv_ref[...], h, _NT, preferred_element_type=F32)

    fl = yv[nq:nq + F_ROWS] + fb_ref[...]
    lf = jnp.minimum(fl, 0.0) - jnp.log(1.0 + jnp.exp(-jnp.abs(fl)))
    cs = _dot(jnp.concatenate(_split3(lf), axis=0), u_ref[...])
    fcum = cs[0:F_ROWS] + cs[F_ROWS:2 * F_ROWS] + cs[2 * F_ROWS:3 * F_ROWS] + carry_ref[:, 0:1]
    carry_ref[...] = jnp.broadcast_to(fcum[:, t - 1:t], carry_ref.shape)

    f2 = fcum * LOG2E
    row8 = lax.broadcasted_iota(jnp.int32, (8, t), 0)
    ones3 = jnp.where(row8 < 3, 1.0, 0.0)
    pad = jnp.zeros((AUG - HEAD_DIM - 16, t), F32)
    q_rows, k_rows = [], []
    for hh in range(N_HEADS):
        hi, mid, lo = (p.astype(F32) for p in _split3(f2[hh:hh + 1]))
        pieces = jnp.where(row8 == 0, hi, jnp.where(row8 == 1, mid, jnp.where(row8 == 2, lo, 0.0)))
        feat = slice(n3 + HEAD_DIM * hh, n3 + HEAD_DIM * (hh + 1))
        q_rows += [yq[feat], ones3, pieces, pad]
        k_rows += [yk[feat], -pieces, ones3, pad]
    qv_ref[0, 0, 0:N_HEADS * AUG + n3, :] = jnp.concatenate(q_rows + [yq[0:n3]], axis=0).astype(BF16)
    qv_ref[0, 0, N_HEADS * AUG + n3:N_HEADS * AUG + n3 + nq, :] = yv[0:nq].astype(BF16)
    ksq = jnp.square(yk.astype(BF16).astype(F32))
    head_rows = ([(DIFF_DIM * i, DIFF_DIM) for i in range(2 * N_HEADS)]
                 + [(2 * BRANCH_WIDTH + HEAD_DIM * i, HEAD_DIM) for i in range(N_HEADS)]
                 + [(n3 + HEAD_DIM * i, HEAD_DIM) for i in range(N_HEADS)])
    kn_ref[0] = jnp.concatenate([jnp.sum(ksq[r0:r0 + w], axis=0, keepdims=True) for r0, w in head_rows], axis=0)
    for half in range(2):
        vs_ref[0, half] = yv[BRANCH_WIDTH:2 * BRANCH_WIDTH, half * (t // 2):(half + 1) * (t // 2)].astype(BF16)
    k_ref[0] = jnp.concatenate(k_rows + [yk[0:n3]], axis=0).T.astype(BF16)


def _projection(x, sc, sh, wq, wk, wv, fb, cos, sin, l):
    b, s, d = x.shape
    t = SEQ_TILE
    ns = s // t
    nq = 4 * BRANCH_WIDTH
    nk = N_HEADS * AUG + 3 * BRANCH_WIDTH
    u = jnp.asarray(np.triu(np.ones((t, t), np.float32)), BF16)
    const = lambda shape: pl.BlockSpec(shape, lambda i, j: (0,) * len(shape))
    return pl.pallas_call(
        _proj_kernel,
        grid=(b, ns),
        in_specs=[
            pl.BlockSpec((1, t, d), lambda i, j: (i, j, 0)),
            pl.BlockSpec((1, 1, 1, d), lambda i, j: (i, 1, 0, 0)),
            pl.BlockSpec((1, 1, 1, d), lambda i, j: (i, 0, 0, 0)),
            const((nq, d)), const((nq, d)), const((nq + F_ROWS, d)),
            const((F_ROWS, 1)),
            pl.BlockSpec((1, 16, t), lambda i, j: (i, 0, j)),
            pl.BlockSpec((1, 16, t), lambda i, j: (i, 0, j)),
            const((t, t)),
        ],
        out_specs=[
            pl.BlockSpec((1, 1, nk + nq, t), lambda i, j: (i, j, 0, 0)),
            pl.BlockSpec((1, t, nk), lambda i, j: (i, j, 0)),
            pl.BlockSpec((1, 2, BRANCH_WIDTH, t // 2), lambda i, j: (i, j, 0, 0)),
            pl.BlockSpec((1, KN_ROWS, t), lambda i, j: (i, 0, j)),
        ],
        out_shape=[
            jax.ShapeDtypeStruct((b, ns, nk + nq, t), BF16),
            jax.ShapeDtypeStruct((b, s, nk), BF16),
            jax.ShapeDtypeStruct((b, 2 * ns, BRANCH_WIDTH, t // 2), BF16),
            jax.ShapeDtypeStruct((b, KN_ROWS, s), F32),
        ],
        scratch_shapes=[pltpu.VMEM((F_ROWS, LANES), F32)],
        compiler_params=_params("arbitrary", "arbitrary"),
        name=f"projection_l{l}",
    )(x, sc, sh, wq, wk, wv, fb, cos, sin, u)


def _masked_q(q_ref, head, width):
    per_group = LANES // width
    g = head // per_group
    qg = q_ref[0, 0, LANES * g:LANES * (g + 1), :]
    rows = lax.broadcasted_iota(jnp.int32, qg.shape, 0)
    r0 = width * (head % per_group)
    return jnp.where((rows >= r0) & (rows < r0 + width), qg, jnp.zeros_like(qg))


def _load_k(k_ref, kb, group, t):
    k0 = pl.multiple_of(kb * t, t)
    return k_ref[0, pl.ds(k0, t), LANES * group:LANES * (group + 1)]


def _load_v(v_ref, kb, head):
    return v_ref[0, kb, HEAD_DIM * head:HEAD_DIM * (head + 1), :]


def _softmax_step(s, carry, vblk):
    m, acc = carry
    t = s.shape[1]
    m_new = jnp.maximum(m, jnp.max(s, axis=0, keepdims=True))
    p = jnp.exp2(s - m_new).astype(BF16)
    alpha = jnp.exp2(m - m_new)
    vaug = jnp.concatenate([vblk, jnp.ones((ONES_ROWS, t), BF16)], axis=0)
    return m_new, alpha * acc + _dot(vaug, p)


def _causal_tile(t, strict):
    key = lax.broadcasted_iota(jnp.int32, (t, t), 0)
    qry = lax.broadcasted_iota(jnp.int32, (t, t), 1)
    return key < qry if strict else key <= qry


def _staggered(n, score_fn, step_fn, first=None, before_last=None):
    out = []
    s_next = score_fn(0) if first is None else first
    for h in range(n):
        s = s_next
        if h + 1 < n:
            s_next = score_fn(h + 1)
        elif before_last is not None:
            before_last()
        out.append(step_fn(h, s))
    return tuple(out)


def _run_chains(chains, first=None, before_last=None):
    _staggered(len(chains), lambda c: chains[c][0](), lambda c, s: chains[c][1](s), first, before_last)


def _interleave(major, minor):
    out, placed = [], 0
    for i, chain in enumerate(major):
        out.append(chain)
        upto = (i + 1) * len(minor) // len(major)
        out += minor[placed:upto]
        placed = upto
    return out


def _checked_blocks(n, score, adjust, vblock, scratch, lo, qb, t):
    s0_ref, m_ref, acc_ref = scratch
    ones = jnp.ones((ONES_ROWS, t), BF16)

    def prefetch_before(kb):
        def prefetch():
            s0_ref[...] = score(jnp.maximum(kb - 1, 0), 0)
        return prefetch

    def loop_body(i, _):
        kb = qb - 1 - i
        excess = []

        def fast_step(h, s):
            m = m_ref[h]
            s = adjust(kb, h, s, False)
            excess.append(jnp.max(s, axis=0, keepdims=True) - m)
            return _dot(jnp.concatenate([vblock(kb, h), ones], axis=0), jnp.exp2(s - m).astype(BF16))

        deltas = _staggered(n, lambda h: score(kb, h), fast_step, first=s0_ref[...],
                            before_last=prefetch_before(kb))
        over = jnp.max(functools.reduce(jnp.maximum, excess)) > SOFTMAX_MARGIN

        @pl.when(jnp.logical_not(over))
        def _():
            for h in range(n):
                acc_ref[h] += deltas[h]

        @pl.when(over)
        def _():
            def exact_step(h, s):
                m_ref[h], acc_ref[h] = _softmax_step(adjust(kb, h, s, False), (m_ref[h], acc_ref[h]), vblock(kb, h))

            _staggered(n, lambda h: score(kb, h), exact_step)

        return 0

    lax.fori_loop(0, qb - lo, loop_body, 0)


def _softmax_blocks(n, score, adjust, vblock, scratch, lo, qb, t, bounds):
    s0_ref, m_ref, acc_ref = scratch
    ones = jnp.ones((ONES_ROWS, t), BF16)

    def prefetch_before(kb):
        def prefetch():
            s0_ref[...] = score(jnp.maximum(kb - 1, 0), 0)
        return prefetch

    def first_step(h, s):
        s = adjust(qb, h, s, True)
        m = jnp.max(s, axis=0, keepdims=True)
        m_ref[h] = m
        acc_ref[h] = _dot(jnp.concatenate([vblock(qb, h), ones], axis=0), jnp.exp2(s - m).astype(BF16))

    _staggered(n, lambda h: score(qb, h), first_step, before_last=prefetch_before(qb))

    def unchecked(kbs):
        chains = [(kb, h) for kb in kbs for h in range(n)]

        def step(c, s):
            kb, h = chains[c]
            p = jnp.exp2(adjust(kb, h, s, False) - m_ref[h]).astype(BF16)
            acc_ref[h] += _dot(jnp.concatenate([vblock(kb, h), ones], axis=0), p)

        _staggered(len(chains), lambda c: score(*chains[c]), step, first=s0_ref[...],
                   before_last=prefetch_before(kbs[-1]))

    safe = jnp.max(functools.reduce(jnp.maximum, [bounds[h] - m_ref[h] for h in range(n)])) <= SOFTMAX_MARGIN
    n_off = qb - lo
    odd = jnp.bitwise_and(n_off, 1)

    @pl.when(jnp.logical_and(safe, odd == 1))
    def _():
        unchecked([qb - 1])

    @pl.when(safe)
    def _():
        def pair(i, _):
            kb = qb - 1 - odd - 2 * i
            unchecked([kb, kb - 1])
            return 0

        lax.fori_loop(0, lax.shift_right_logical(n_off, 1), pair, 0)

    @pl.when(jnp.logical_not(safe))
    def _():
        _checked_blocks(n, score, adjust, vblock, scratch, lo, qb, t)

    return [acc_ref[h, 0:HEAD_DIM] / acc_ref[h, HEAD_DIM:HEAD_DIM + 1] for h in range(n)]


def _score_bounds(q_heads, kn_ref, row0, extra):
    out = []
    for h, q in enumerate(q_heads):
        qn2 = jnp.sum(jnp.square(q.astype(F32)), axis=0, keepdims=True)
        kn2 = jnp.max(kn_ref[0, row0 + h:row0 + h + 1, :], axis=1, keepdims=True)
        out.append(jnp.sqrt(qn2 * kn2) * 1.001 + (1.0 + extra))
    return out


def _diff_stick_kernel(qd_ref, kd_ref, vd_ref, kn_ref, lam_ref, g_ref, qs_ref, ks_ref, vs_ref, ut_ref,
                       od_ref, os_ref, s0_ref, m_ref, acc_ref, run_ref, sacc_ref, *, lambda_init):
    qb = pl.program_id(1)
    t = qd_ref.shape[-1]
    sub = t // 2
    n = 2 * N_HEADS
    gd, gs = LANES // DIFF_DIM, LANES // HEAD_DIM
    ones = jnp.ones((ONES_ROWS, t), BF16)
    lam_p = lam_ref[...]
    lam = (jnp.exp(jnp.sum(lam_p[0:1] * lam_p[1:2], keepdims=True))
           - jnp.exp(jnp.sum(lam_p[2:3] * lam_p[3:4], keepdims=True)) + lambda_init)
    qmd = [_masked_q(qd_ref, head, DIFF_DIM) for head in range(n)]
    qms = [_masked_q(qs_ref, head, HEAD_DIM) for head in range(N_HEADS)]
    run_ref[...] = jnp.zeros_like(run_ref)
    sacc_ref[...] = jnp.zeros_like(sacc_ref)

    def d_score(kb, head):
        return _dot(_load_k(kd_ref, kb, head // gd, t), qmd[head])

    def d_adjust(kb, head, s, diag):
        return jnp.where(_causal_tile(t, False), s, NEG_BIG) if diag else s

    def d_vblock(kb, head):
        return _load_v(vd_ref, kb, head // 2)

    def d_prefetch(kb):
        def prefetch():
            s0_ref[...] = d_score(jnp.maximum(kb - 1, 0), 0)
        return prefetch

    def d_first(head):
        def step(s):
            s = d_adjust(qb, head, s, True)
            m = jnp.max(s, axis=0, keepdims=True)
            m_ref[head] = m
            acc_ref[head] = _dot(jnp.concatenate([d_vblock(qb, head), ones], axis=0), jnp.exp2(s - m).astype(BF16))
        return step

    def d_unchecked(kb, head):
        def step(s):
            p = jnp.exp2(s - m_ref[head]).astype(BF16)
            acc_ref[head] += _dot(jnp.concatenate([d_vblock(kb, head), ones], axis=0), p)
        return step

    def d_chains(kb, step_of):
        return [(functools.partial(d_score, kb, head), step_of(head)) for head in range(n)]

    def s_chains(hb, q0, key_off):
        cols = slice(q0, t)
        k0 = pl.multiple_of(hb * sub, sub)
        if key_off is not None:
            key = lax.broadcasted_iota(jnp.int32, (sub, t - q0), 0) + key_off
            qry = lax.broadcasted_iota(jnp.int32, (sub, t - q0), 1) + q0
            valid = key < qry

        stash = {}

        def score_a(head):
            g = head // gs
            return _dot(ks_ref[0, pl.ds(k0, sub), LANES * g:LANES * (g + 1)], qms[head][:, cols])

        def step_a(head, z):
            keep = jnp.maximum(z, 0.0) + jnp.log(1.0 + jnp.exp2(jnp.abs(z) * -LOG2E))
            if key_off is not None:
                keep = jnp.where(valid, keep, 0.0)
            keep = keep.astype(BF16)
            stash[head] = (z - keep.astype(F32), keep)

        def score_b(head):
            return _dot(ut_ref[...], stash[head][1])

        def step_b(head, later):
            log_sig, keep = stash[head]
            a = jnp.exp(log_sig - later + run_ref[head, :, cols])
            if key_off is not None:
                a = jnp.where(valid, a, 0.0)
            sacc_ref[head, :, cols] += _dot(vs_ref[0, hb, HEAD_DIM * head:HEAD_DIM * (head + 1), :], a.astype(BF16))
            run_ref[head, :, cols] -= later[0:1] + keep[0:1].astype(F32)

        a_chains = [(functools.partial(score_a, head), functools.partial(step_a, head)) for head in range(N_HEADS)]
        b_chains = [(functools.partial(score_b, head), functools.partial(step_b, head)) for head in range(N_HEADS)]
        order = [a_chains[0]]
        for head in range(1, N_HEADS):
            order += [a_chains[head], b_chains[head - 1]]
        return order + [b_chains[N_HEADS - 1]]

    def alive():
        worst = functools.reduce(jnp.maximum, [run_ref[head] for head in range(N_HEADS)])
        return jnp.max(worst) > STICK_EXIT

    _run_chains(_interleave(d_chains(qb, lambda head: d_first(head)),
                            s_chains(2 * qb + 1, sub, sub) + s_chains(2 * qb, 0, 0)),
                before_last=d_prefetch(qb))

    bounds = _score_bounds(qmd, kn_ref, 0, 0.0)
    safe = jnp.max(functools.reduce(jnp.maximum, [bounds[h] - m_ref[h] for h in range(n)])) <= SOFTMAX_MARGIN
    n_fused = jnp.minimum(qb, FUSED_STICK_STEPS)

    @pl.when(safe)
    def _():
        def fused(i, _):
            kb = qb - 1 - i
            _run_chains(_interleave(d_chains(kb, lambda head: d_unchecked(kb, head)),
                                    s_chains(2 * qb - 1 - i, 0, None)),
                        first=s0_ref[...], before_last=d_prefetch(kb))
            return 0

        def plain(i, _):
            kb = qb - 1 - i
            _run_chains(d_chains(kb, lambda head: d_unchecked(kb, head)), first=s0_ref[...],
                        before_last=d_prefetch(kb))
            return 0

        lax.fori_loop(0, n_fused, fused, 0)
        lax.fori_loop(n_fused, qb, plain, 0)

    @pl.when(jnp.logical_not(safe))
    def _():
        _checked_blocks(n, d_score, d_adjust, d_vblock, (s0_ref, m_ref, acc_ref), 0, qb, t)

    def earlier(state):
        i, _ = state
        _run_chains(s_chains(2 * qb - 1 - i, 0, None))
        return i + 1, alive()

    lax.while_loop(lambda st: jnp.logical_and(st[0] < 2 * qb, st[1]), earlier,
                   (jnp.where(safe, n_fused, 0), alive()))
    os_ref[0] = jnp.concatenate([sacc_ref[head] for head in range(N_HEADS)], axis=0).T.astype(BF16)

    comps = [acc_ref[h, 0:HEAD_DIM] / acc_ref[h, HEAD_DIM:HEAD_DIM + 1] for h in range(n)]
    outs = []
    for hh in range(N_HEADS):
        o = comps[2 * hh] - lam * comps[2 * hh + 1]
        o = o * lax.rsqrt(jnp.mean(o * o, axis=0, keepdims=True) + LN_EPS)
        outs.append(o * (g_ref[...] * (1.0 - lambda_init)))
    od_ref[0] = jnp.concatenate(outs, axis=0).T.astype(BF16)


def _dilated_kernel(q_ref, k_ref, v_ref, kn_ref, bias_ref, o_ref, *scratch):
    qb = pl.program_id(1)
    t = q_ref.shape[-1]
    n_back = bias_ref.shape[0] - 1
    per_group = LANES // HEAD_DIM
    qms = [_masked_q(q_ref, head, HEAD_DIM) for head in range(N_HEADS)]

    def score(kb, head):
        return _dot(_load_k(k_ref, kb, head // per_group, t), qms[head])

    def adjust(kb, head, s, diag):
        return bias_ref[qb - kb] + s

    outs = _softmax_blocks(N_HEADS, score, adjust, lambda kb, head: _load_v(v_ref, kb, head), scratch,
                           jnp.maximum(qb - n_back, 0), qb, t,
                           _score_bounds(qms, kn_ref, 2 * N_HEADS, math.log2(len(DILATED_PATTERNS))))
    o_ref[0] = jnp.concatenate(outs, axis=0).T.astype(BF16)


def _forget_kernel(q_ref, k_ref, v_ref, kn_ref, o_ref, *scratch):
    qb = pl.program_id(1)
    t = q_ref.shape[-1]
    qas = [q_ref[0, 0, AUG * head:AUG * (head + 1), :] for head in range(N_HEADS)]

    def score(kb, head):
        return _dot(_load_k(k_ref, kb, head, t), qas[head])

    def adjust(kb, head, s, diag):
        return jnp.where(_causal_tile(t, False), s, NEG_BIG) if diag else s

    bounds = _score_bounds([qa[0:HEAD_DIM] for qa in qas], kn_ref, 3 * N_HEADS, 0.0)
    outs = _softmax_blocks(N_HEADS, score, adjust, lambda kb, head: _load_v(v_ref, kb, head), scratch, 0, qb, t,
                           bounds)
    o_ref[0] = jnp.concatenate(outs, axis=0).T.astype(BF16)


def _dilated_bias(t):
    n_back = max(w for w, _ in DILATED_PATTERNS) // t
    kk = np.arange(t)[:, None]
    qq = np.arange(t)[None, :]
    tiles = []
    for d in range(n_back + 1):
        dist = d * t + qq - kk
        count = np.zeros((t, t), np.float64)
        for window, dil in DILATED_PATTERNS:
            count += (dist >= 0) & (dist % dil == 0) & (dist // dil <= window // dil)
        tiles.append(np.where(count > 0, np.log2(np.maximum(count, 1.0)), NEG_BIG))
    return jnp.asarray(np.stack(tiles), F32)


def _attention(kind, branch, qv, k, extra_inputs, extra_specs, kernel, l, scratch=()):
    b, ns, _, t = qv.shape
    s = ns * t
    n_aug = N_HEADS * AUG
    if branch == 3:
        width, qk_block = n_aug, 0
    else:
        width, qk_block = BRANCH_WIDTH, n_aug // BRANCH_WIDTH + branch
    v_block = (n_aug + 3 * BRANCH_WIDTH) // BRANCH_WIDTH + branch
    in_specs = [
        pl.BlockSpec((1, 1, width, t), lambda i, j: (i, j, qk_block, 0)),
        pl.BlockSpec((1, s, width), lambda i, j: (i, 0, qk_block)),
        pl.BlockSpec((1, ns, BRANCH_WIDTH, t), lambda i, j: (i, 0, v_block, 0)),
    ] + extra_specs
    return pl.pallas_call(
        kernel,
        grid=(b, ns),
        in_specs=in_specs,
        out_specs=pl.BlockSpec((1, t, BRANCH_WIDTH), lambda i, j: (i, j, 0)),
        out_shape=jax.ShapeDtypeStruct((b, s, BRANCH_WIDTH), BF16),
        scratch_shapes=list(scratch),
        compiler_params=_params("arbitrary", "arbitrary"),
        name=f"{kind}_attention_l{l}",
    )(qv, k, qv, *extra_inputs)


def _diff_stick_attention(qv, k, vs, kn, lam, gain, ut, lambda_init, l):
    b, ns, _, t = qv.shape
    s = ns * t
    first = N_HEADS * AUG // BRANCH_WIDTH
    v_block = (N_HEADS * AUG + 3 * BRANCH_WIDTH) // BRANCH_WIDTH
    const = lambda shape: pl.BlockSpec(shape, lambda i, j: (0,) * len(shape))
    q_spec = lambda blk: pl.BlockSpec((1, 1, BRANCH_WIDTH, t), lambda i, j: (i, j, blk, 0))
    k_spec = lambda blk: pl.BlockSpec((1, s, BRANCH_WIDTH), lambda i, j: (i, 0, blk))
    o_spec = pl.BlockSpec((1, t, BRANCH_WIDTH), lambda i, j: (i, j, 0))
    n = 2 * N_HEADS
    return pl.pallas_call(
        functools.partial(_diff_stick_kernel, lambda_init=lambda_init),
        grid=(b, ns),
        in_specs=[q_spec(first), k_spec(first),
                  pl.BlockSpec((1, ns, BRANCH_WIDTH, t), lambda i, j: (i, 0, v_block, 0)),
                  pl.BlockSpec((1, KN_ROWS, s), lambda i, j: (i, 0, 0)), const(lam.shape), const(gain.shape),
                  q_spec(first + 1), k_spec(first + 1),
                  pl.BlockSpec((1, 2 * ns, BRANCH_WIDTH, t // 2), lambda i, j: (i, 0, 0, 0)), const(ut.shape)],
        out_specs=[o_spec, o_spec],
        out_shape=[jax.ShapeDtypeStruct((b, s, BRANCH_WIDTH), BF16)] * 2,
        scratch_shapes=[pltpu.VMEM((t, t), F32), pltpu.VMEM((n, 1, t), F32),
                        pltpu.VMEM((n, HEAD_DIM + ONES_ROWS, t), F32),
                        pltpu.VMEM((N_HEADS, 1, t), F32), pltpu.VMEM((N_HEADS, HEAD_DIM, t), F32)],
        compiler_params=_params("arbitrary", "arbitrary"),
        name=f"diff_stick_attention_l{l}",
    )(qv, k, qv, kn, lam, gain, qv, k, vs, ut)


def _merge_kernel(x_ref, sc_ref, sh_ref, ga_ref, oa_ref, ob_ref, oc_ref, od_ref, wg_ref, wb_ref, wo_ref,
                  lng_ref, lnb_ref, out_ref):
    t, d = x_ref.shape[1], x_ref.shape[2]
    for r0 in range(0, t, t // ROW_GROUPS):
        rows = slice(r0, r0 + t // ROW_GROUPS)
        x = x_ref[0, rows]
        h = (x * (1.0 + sc_ref[0, 0]) + sh_ref[0, 0]).astype(BF16)
        merged = None
        for n, o_ref in enumerate((oa_ref, ob_ref, oc_ref, od_ref)):
            gate = 1.0 / (1.0 + jnp.exp(-_dot(h, wg_ref[:, d * n:d * (n + 1)])))
            y = gate * _dot(o_ref[0, rows], wb_ref[n])
            merged = y if merged is None else merged + y
        mix = _dot(merged.astype(BF16), wo_ref[...])
        out_ref[0, rows] = _layer_norm(DEEPNORM_ALPHA * x + ga_ref[0, 0] * mix, lng_ref[...], lnb_ref[...])


def _merge(x, mod, outs, wg, wb, wo, ln_g, ln_b, l):
    b, s, d = x.shape
    t = SEQ_TILE
    const = lambda shape: pl.BlockSpec(shape, lambda i, j: (0,) * len(shape))
    mod_spec = lambda idx: pl.BlockSpec((1, 1, 1, d), lambda i, j: (i, idx, 0, 0))
    o_spec = pl.BlockSpec((1, t, BRANCH_WIDTH), lambda i, j: (i, j, 0))
    return pl.pallas_call(
        _merge_kernel,
        grid=(b, s // t),
        in_specs=[pl.BlockSpec((1, t, d), lambda i, j: (i, j, 0)), mod_spec(1), mod_spec(0), mod_spec(2),
                  o_spec, o_spec, o_spec, o_spec,
                  const((d, N_BRANCHES * d)), const((N_BRANCHES, BRANCH_WIDTH, d)), const((d, d)),
                  const((1, d)), const((1, d))],
        out_specs=pl.BlockSpec((1, t, d), lambda i, j: (i, j, 0)),
        out_shape=jax.ShapeDtypeStruct((b, s, d), F32),
        compiler_params=_params("arbitrary", "arbitrary"),
        name=f"merge_l{l}",
    )(x, mod, mod, mod, *outs, wg, wb, wo, ln_g.reshape(1, d), ln_b.reshape(1, d))


def _ffn_kernel(x_ref, sc_ref, sh_ref, gf_ref, wup_ref, cw_ref, cb_ref, wdn_ref, lng_ref, lnb_ref,
                out_ref, tail_ref):
    t = x_ref.shape[1]
    chunk = D_FF // 2

    @pl.when(pl.program_id(1) == 0)
    def _():
        tail_ref[...] = jnp.zeros_like(tail_ref)

    x = x_ref[0]
    h = (x * (1.0 + sc_ref[0, 0]) + sh_ref[0, 0]).astype(BF16)
    rows = lax.broadcasted_iota(jnp.int32, (8, chunk), 0)

    def conv(col0):
        cols = slice(col0, col0 + chunk)
        u = _dot(h, wup_ref[:, cols])
        tail = tail_ref[:, cols]
        u1, u2 = pltpu.roll(u, 1, 0), pltpu.roll(u, 2, 0)
        head1 = jnp.where(rows < 1, pltpu.roll(tail, 1, 0), u1[0:8])
        head2 = jnp.where(rows < 2, pltpu.roll(tail, 2, 0), u2[0:8])
        u1 = jnp.concatenate([head1, u1[8:]], axis=0)
        u2 = jnp.concatenate([head2, u2[8:]], axis=0)
        tail_ref[:, cols] = u[t - 8:t]
        return cw_ref[2:3, cols] * u + cw_ref[1:2, cols] * u1 + cw_ref[0:1, cols] * u2 + cb_ref[:, cols]

    ffn = None
    for c in range(D_FF // chunk):
        a = conv(c * chunk)
        g = conv(D_FF + c * chunk)
        act = (a / (1.0 + jnp.exp(-a)) * g).astype(BF16)
        y = _dot(act, wdn_ref[c * chunk:(c + 1) * chunk, :])
        ffn = y if ffn is None else ffn + y
    out_ref[0] = _layer_norm(DEEPNORM_ALPHA * x + gf_ref[0, 0] * ffn, lng_ref[...], lnb_ref[...])


def _conv_ffn(x, mod, wup, cw, cb, wdn, ln_g, ln_b, l):
    b, s, d = x.shape
    t = SEQ_TILE
    const = lambda shape: pl.BlockSpec(shape, lambda i, j: (0,) * len(shape))
    mod_spec = lambda idx: pl.BlockSpec((1, 1, 1, d), lambda i, j: (i, idx, 0, 0))
    return pl.pallas_call(
        _ffn_kernel,
        grid=(b, s // t),
        in_specs=[pl.BlockSpec((1, t, d), lambda i, j: (i, j, 0)), mod_spec(4), mod_spec(3), mod_spec(5),
                  const((d, 2 * D_FF)), const((CONV_WIDTH, 2 * D_FF)), const((1, 2 * D_FF)), const((D_FF, d)),
                  const((1, d)), const((1, d))],
        out_specs=pl.BlockSpec((1, t, d), lambda i, j: (i, j, 0)),
        out_shape=jax.ShapeDtypeStruct((b, s, d), F32),
        scratch_shapes=[pltpu.VMEM((8, 2 * D_FF), F32)],
        compiler_params=_params("arbitrary", "arbitrary"),
        name=f"conv_ffn_l{l}",
    )(x, mod, mod, mod, wup, cw, cb.reshape(1, 2 * D_FF), wdn, ln_g.reshape(1, d), ln_b.reshape(1, d))


def kernel(x, c, positions, w_ada, b_ada, w_in, lam_q1, lam_k1, lam_q2, lam_k2, subln_g, forget_b, w_branch,
           w_o, ln1_g, ln1_b, w_up, conv_w, conv_b, w_down, ln2_g, ln2_b):
    b, s, d = x.shape
    t = SEQ_TILE
    bw = BRANCH_WIDTH
    const = lambda shape: pl.BlockSpec(shape, lambda i, j: (0,) * len(shape))

    mod = _modulation(c, w_ada, b_ada)
    cos, sin = _rope_tables(positions)
    bias = _dilated_bias(t)
    sub = t // 2
    ut = jnp.asarray(np.triu(np.ones((sub, sub), np.float32), 1), BF16)
    chains = lambda n: [pltpu.VMEM((t, t), F32), pltpu.VMEM((n, 1, t), F32),
                        pltpu.VMEM((n, HEAD_DIM + ONES_ROWS, t), F32)]

    for l in range(DEPTH):
        lambda_init = 0.8 - 0.6 * math.exp(-0.3 * l)
        w = w_in[l]
        col = lambda i: w[:, bw * i:bw * (i + 1)]
        wq = jnp.concatenate([col(0) * (DIFF_DIM ** -0.5 * LOG2E), col(3) * HEAD_DIM ** -0.5,
                              col(6) * (HEAD_DIM ** -0.5 * LOG2E), col(9) * (HEAD_DIM ** -0.5 * LOG2E)],
                             axis=1).T.astype(BF16)
        wk = jnp.concatenate([col(1), col(4), col(7), col(10)], axis=1).T.astype(BF16)
        wf = jnp.zeros((d, F_ROWS), F32).at[:, :N_HEADS].set(w[:, 12 * bw:12 * bw + N_HEADS])
        wv = jnp.concatenate([col(2), col(5), col(8), col(11), wf], axis=1).T.astype(BF16)
        wg = w[:, 12 * bw + N_HEADS:].astype(BF16)
        fb = jnp.zeros((F_ROWS, 1), F32).at[:N_HEADS, 0].set(forget_b[l])

        qv, k, vs, kn = _projection(x, mod[l], mod[l], wq, wk, wv, fb, cos, sin, l)

        lam = jnp.stack([lam_q1[l], lam_k1[l], lam_q2[l], lam_k2[l]])
        kn_spec = pl.BlockSpec((1, KN_ROWS, s), lambda i, j: (i, 0, 0))
        oa, ob = _diff_stick_attention(qv, k, vs, kn, lam, subln_g[l].reshape(HEAD_DIM, 1), ut, lambda_init, l)
        oc = _attention("dilated", 2, qv, k, [kn, bias], [kn_spec, const(bias.shape)], _dilated_kernel, l,
                        chains(N_HEADS))
        od = _attention("forget", 3, qv, k, [kn], [kn_spec], _forget_kernel, l, chains(N_HEADS))

        x = _merge(x, mod[l], (oa, ob, oc, od), wg, w_branch[l].astype(BF16), w_o[l].astype(BF16),
                   ln1_g[l], ln1_b[l], l)
        x = _conv_ffn(x, mod[l], w_up[l].astype(BF16), conv_w[l], conv_b[l], w_down[l].astype(BF16),
                      ln2_g[l], ln2_b[l], l)
    return x
```
